```python
import jax, jax.numpy as jnp
from jax import lax
import numpy as np

D_MODEL = 1024
BATCH = 2
SEQ = 8192
DEPTH = 4
DEC_BATCH = 32
DEC_SEQ = 4
PAST_LEN = 8192
PAGE_SIZE = 128

N_META = 16
ATT_HEAD_DIM = 64
ATT_WIDTH = D_MODEL // 2
ATT_HEADS = ATT_WIDTH // ATT_HEAD_DIM
HG_HEAD_DIM = 128
HG_WIDTH = D_MODEL - ATT_WIDTH
HG_HEADS = HG_WIDTH // HG_HEAD_DIM
MIX_WIDTH = ATT_WIDTH + HG_WIDTH
N_IN = 3 * ATT_WIDTH + ATT_HEADS + 4 * HG_WIDTH
D_FF = 2816
CONV_W = 3
Q_BLOCK = 128
HG_CHUNK = 64
EPS = 1e-6
F32 = jnp.float32

kernel_name = 'hymba_fox_hgrn2_convffn_step'


def rms_norm(x, g):
    xf = x.astype(F32)
    y = xf * lax.rsqrt(jnp.mean(xf * xf, axis=-1, keepdims=True) + EPS)
    return (y * g.astype(F32)).astype(x.dtype)


def hgrn_lower_bounds(hg_lb):
    s = jax.nn.softmax(hg_lb.astype(F32), axis=0)
    return jnp.cumsum(s, axis=0) - s[0:1]


def mixer_inputs(h, w_in, b_f, lb):
    B, T, _ = h.shape
    p = h @ w_in
    A, W = ATT_WIDTH, HG_WIDTH
    ash = (B, T, ATT_HEADS, ATT_HEAD_DIM)
    aq = p[..., 0:A].reshape(ash)
    ak = p[..., A:2 * A].reshape(ash)
    av = p[..., 2 * A:3 * A].reshape(ash)
    zf = p[..., 3 * A:3 * A + ATT_HEADS].astype(F32)
    o = 3 * A + ATT_HEADS
    hq = p[..., o:o + W].astype(F32)
    hf = p[..., o + W:o + 2 * W].astype(F32)
    hi = p[..., o + 2 * W:o + 3 * W].astype(F32)
    hg = p[..., o + 3 * W:o + 4 * W]
    att_logf = jax.nn.log_sigmoid(zf + b_f.astype(F32))
    f = lb + (1.0 - lb) * jax.nn.sigmoid(hf)
    hsh = (B, T, HG_HEADS, HG_HEAD_DIM)
    hq = (hq * (HG_HEAD_DIM ** -0.5)).reshape(hsh)
    return (aq, ak, av, att_logf, hq, (1.0 - f).reshape(hsh), hi.reshape(hsh),
            jnp.log(f).reshape(hsh), hg.reshape(hsh))


def fox_prompt(q, k, v, logf):
    B, L, H, Dh = q.shape
    scale = Dh ** -0.5
    cum = jnp.cumsum(logf, axis=1).transpose(0, 2, 1)
    key_pos = jnp.arange(L)

    def attend(q_blk, c_blk, pos):
        s = jnp.einsum('bqhd,bkhd->bhqk', q_blk, k).astype(F32) * scale
        s = s + c_blk[..., None] - cum[:, :, None, :]
        s = jnp.where(key_pos[None, :] <= pos[:, None], s, -jnp.inf)
        p = jax.nn.softmax(s, axis=-1)
        return jnp.einsum('bhqk,bkhd->bqhd', p.astype(v.dtype), v)

    meta_out = attend(q[:, :N_META], cum[:, :, :N_META], key_pos[:N_META])
    nb = (L - N_META) // Q_BLOCK
    qb = q[:, N_META:].reshape(B, nb, Q_BLOCK, H, Dh).transpose(1, 0, 2, 3, 4)
    cb = cum[:, :, N_META:].reshape(B, H, nb, Q_BLOCK).transpose(2, 0, 1, 3)
    pb = key_pos[N_META:].reshape(nb, Q_BLOCK)
    out = lax.map(lambda a: attend(a[0], a[1], a[2]), (qb, cb, pb))
    out = out.transpose(1, 0, 2, 3, 4).reshape(B, L - N_META, H, Dh)
    return jnp.concatenate([meta_out, out], axis=1)


def fox_sample(q, k, v, logf, k_past, v_past, lf_past):
    Dh = q.shape[-1]
    T = q.shape[1]
    P = k_past.shape[1]
    scale = Dh ** -0.5
    cn = jnp.cumsum(logf, axis=1).transpose(0, 2, 1)
    lfp = lf_past.astype(F32)
    r = (lax.cumsum(lfp, axis=1, reverse=True) - lfp).transpose(0, 2, 1)
    s_past = jnp.einsum('bqhd,bkhd->bhqk', q, k_past).astype(F32) * scale
    s_past = s_past + cn[..., None] + r[:, :, None, :]
    s_new = jnp.einsum('bqhd,bkhd->bhqk', q, k).astype(F32) * scale
    s_new = s_new + cn[..., None] - cn[:, :, None, :]
    causal = jnp.arange(T)[None, :] <= jnp.arange(T)[:, None]
    s_new = jnp.where(causal, s_new, -jnp.inf)
    p = jax.nn.softmax(jnp.concatenate([s_past, s_new], axis=-1), axis=-1).astype(v.dtype)
    return (jnp.einsum('bhqk,bkhd->bqhd', p[..., :P], v_past.astype(v.dtype))
            + jnp.einsum('bhqk,bkhd->bqhd', p[..., P:], v))


def hgrn_chunk(q, k, v, logf, S0):
    C = q.shape[2]
    cum = jnp.cumsum(logf, axis=2)
    o_inter = jnp.einsum('bhck,bhkv->bhcv', q * jnp.exp(cum), S0)
    causal = jnp.tril(jnp.ones((C, C), dtype=bool))
    diff = cum[:, :, :, None, :] - cum[:, :, None, :, :]
    decay = jnp.exp(jnp.where(causal[:, :, None], diff, -jnp.inf))
    a = jnp.einsum('bhtk,bhsk,bhtsk->bhts', q, k, decay)
    o = o_inter + jnp.einsum('bhts,bhsv->bhtv', a, v)
    last = cum[:, :, -1:, :]
    S1 = (jnp.exp(last[:, :, 0, :, None]) * S0
          + jnp.einsum('bhsk,bhsv->bhkv', k * jnp.exp(last - cum), v))
    return o, S1


def hgrn_prompt(q, k, v, logf):
    B, L, H, K = q.shape
    q, k, v, logf = [a.transpose(0, 2, 1, 3) for a in (q, k, v, logf)]
    S0 = jnp.zeros((B, H, K, v.shape[-1]), F32)
    o_meta, S = hgrn_chunk(q[:, :, :N_META], k[:, :, :N_META], v[:, :, :N_META],
                           logf[:, :, :N_META], S0)
    nc = (L - N_META) // HG_CHUNK

    def to_chunks(a):
        return a[:, :, N_META:].reshape(B, H, nc, HG_CHUNK, a.shape[-1]).transpose(2, 0, 1, 3, 4)

    def step(S, xs):
        o, S = hgrn_chunk(xs[0], xs[1], xs[2], xs[3], S)
        return S, o

    S, o_real = lax.scan(step, S, (to_chunks(q), to_chunks(k), to_chunks(v), to_chunks(logf)))
    o_real = o_real.transpose(1, 2, 0, 3, 4).reshape(B, H, L - N_META, -1)
    o = jnp.concatenate([o_meta, o_real], axis=2).transpose(0, 2, 1, 3)
    return o, S


def hgrn_sample(q, k, v, logf, S0):
    q, k, v, logf = [a.transpose(0, 2, 1, 3) for a in (q, k, v, logf)]
    o, S = hgrn_chunk(q, k, v, logf, S0.astype(F32))
    return o.transpose(0, 2, 1, 3), S


def merge_heads(att, ho, hg, g_a, g_h, w_o):
    B, T = att.shape[:2]
    a = rms_norm(att.reshape(B, T, ATT_WIDTH), g_a)
    hn = rms_norm(ho.astype(att.dtype), g_h.reshape(HG_HEADS, HG_HEAD_DIM))
    hgo = (hn * jax.nn.silu(hg)).reshape(B, T, HG_WIDTH).astype(a.dtype)
    return jnp.concatenate([a, hgo], axis=-1) @ w_o


def conv_ffn(h, prev, w_up, conv_w, conv_b, w_down):
    u = h @ w_up
    a, b = u[..., :D_FF], u[..., D_FF:]
    T = a.shape[1]
    ap = jnp.concatenate([prev.astype(a.dtype), a], axis=1)
    c = conv_b
    for j in range(CONV_W):
        c = c + conv_w[j] * ap[:, j:j + T]
    y = (jax.nn.silu(c) * b) @ w_down
    return y, ap[:, ap.shape[1] - (CONV_W - 1):]


def setup_inputs(seed: int = 0) -> dict:
    key = jax.random.key(seed)
    ks = jax.random.split(key, 24)
    n_pages = PAST_LEN // PAGE_SIZE
    n_used = DEC_BATCH * n_pages
    n_pool = n_used + (n_used + 3) // 4
    perm = jax.random.permutation(ks[0], n_pool)[:n_used]
    page_table = perm.reshape(DEC_BATCH, n_pages).astype(jnp.int32)
    nrm = jax.random.normal
    return {
        'x_prompt': nrm(ks[1], (BATCH, SEQ, D_MODEL), F32),
        'x_sample': nrm(ks[2], (DEC_BATCH, DEC_SEQ, D_MODEL), F32),
        'cache_k': nrm(ks[3], (DEPTH, n_pool, PAGE_SIZE, ATT_HEADS, ATT_HEAD_DIM), F32),
        'cache_v': nrm(ks[4], (DEPTH, n_pool, PAGE_SIZE, ATT_HEADS, ATT_HEAD_DIM), F32),
        'cache_logf': jax.nn.log_sigmoid(3.0 + nrm(ks[5], (DEPTH, n_pool, PAGE_SIZE, ATT_HEADS), F32)),
        'state_hgrn': 0.5 * nrm(ks[6], (DEPTH, DEC_BATCH, HG_HEADS, HG_HEAD_DIM, HG_HEAD_DIM), F32),
        'state_conv': nrm(ks[7], (DEPTH, DEC_BATCH, CONV_W - 1, D_FF), F32),
        'page_table': page_table,
        'meta_tokens': nrm(ks[8], (N_META, D_MODEL), F32),
        'g_mix': 1.0 + 0.1 * nrm(ks[9], (DEPTH, D_MODEL), F32),
        'w_in': nrm(ks[10], (DEPTH, D_MODEL, N_IN), F32) * D_MODEL ** -0.5,
        'b_f': jnp.linspace(1.0, 5.0, ATT_HEADS, dtype=F32)[None, :] + 0.1 * nrm(ks[11], (DEPTH, ATT_HEADS), F32),
        'hg_lb': 0.1 * nrm(ks[12], (DEPTH, HG_WIDTH), F32),
        'g_att_out': 1.0 + 0.1 * nrm(ks[13], (DEPTH, ATT_WIDTH), F32),
        'g_hg_out': 1.0 + 0.1 * nrm(ks[14], (DEPTH, HG_WIDTH), F32),
        'w_out': nrm(ks[15], (DEPTH, MIX_WIDTH, D_MODEL), F32) * MIX_WIDTH ** -0.5,
        'g_ffn': 1.0 + 0.1 * nrm(ks[16], (DEPTH, D_MODEL), F32),
        'w_up': nrm(ks[17], (DEPTH, D_MODEL, 2 * D_FF), F32) * D_MODEL ** -0.5,
        'conv_w': 0.5 * nrm(ks[18], (DEPTH, CONV_W, D_FF), F32),
        'conv_b': 0.02 * nrm(ks[19], (DEPTH, D_FF), F32),
        'w_down': nrm(ks[20], (DEPTH, D_FF, D_MODEL), F32) * D_FF ** -0.5,
        'g_final': 1.0 + 0.1 * nrm(ks[21], (D_MODEL,), F32),
    }


def reference(x_prompt, x_sample, cache_k, cache_v, cache_logf, state_hgrn, state_conv, page_table,
              meta_tokens, g_mix, w_in, b_f, hg_lb, g_att_out, g_hg_out, w_out, g_ffn, w_up,
              conv_w, conv_b, w_down, g_final):
    B = x_prompt.shape[0]
    Bd = x_sample.shape[0]
    P = page_table.shape[1] * PAGE_SIZE
    lb_all = hgrn_lower_bounds(hg_lb)
    xp = jnp.concatenate([jnp.broadcast_to(meta_tokens[None].astype(x_prompt.dtype),
                                           (B, N_META, D_MODEL)), x_prompt], axis=1)
    xs = x_sample
    kp_l, vp_l, lfp_l, sp_l, cp_l = [], [], [], [], []
    ks_l, vs_l, lfs_l, ss_l, cs_l = [], [], [], [], []
    for l in range(DEPTH):
        hp = rms_norm(xp, g_mix[l])
        aq, ak, av, alf, hq, hk, hv, hlf, hg = mixer_inputs(hp, w_in[l], b_f[l], lb_all[l])
        att = fox_prompt(aq, ak, av, alf)
        ho, S_p = hgrn_prompt(hq, hk, hv, hlf)
        xp = xp + merge_heads(att, ho, hg, g_att_out[l], g_hg_out[l], w_out[l])
        prev0 = jnp.zeros((B, CONV_W - 1, D_FF), xp.dtype)
        y, c_p = conv_ffn(rms_norm(xp, g_ffn[l]), prev0, w_up[l], conv_w[l], conv_b[l], w_down[l])
        xp = xp + y
        kp_l.append(ak); vp_l.append(av); lfp_l.append(alf); sp_l.append(S_p); cp_l.append(c_p)

        hs = rms_norm(xs, g_mix[l])
        sq, sk, sv, slf, shq, shk, shv, shlf, shg = mixer_inputs(hs, w_in[l], b_f[l], lb_all[l])
        k_past = cache_k[l][page_table].reshape(Bd, P, ATT_HEADS, ATT_HEAD_DIM)
        v_past = cache_v[l][page_table].reshape(Bd, P, ATT_HEADS, ATT_HEAD_DIM)
        lf_past = cache_logf[l][page_table].reshape(Bd, P, ATT_HEADS)
        att_s = fox_sample(sq, sk, sv, slf, k_past, v_past, lf_past)
        ho_s, S_s = hgrn_sample(shq, shk, shv, shlf, state_hgrn[l])
        xs = xs + merge_heads(att_s, ho_s, shg, g_att_out[l], g_hg_out[l], w_out[l])
        y_s, c_s = conv_ffn(rms_norm(xs, g_ffn[l]), state_conv[l], w_up[l], conv_w[l], conv_b[l], w_down[l])
        xs = xs + y_s
        ks_l.append(sk); vs_l.append(sv); lfs_l.append(slf); ss_l.append(S_s); cs_l.append(c_s)

    y_prompt = rms_norm(xp, g_final)[:, N_META:]
    y_sample = rms_norm(xs, g_final)
    return (y_prompt, y_sample,
            jnp.stack(kp_l), jnp.stack(vp_l), jnp.stack(lfp_l), jnp.stack(sp_l), jnp.stack(cp_l),
            jnp.stack(ks_l), jnp.stack(vs_l), jnp.stack(lfs_l), jnp.stack(ss_l), jnp.stack(cs_l))
```

```python
import functools

import numpy as np
import jax
import jax.numpy as jnp
from jax import lax
from jax.experimental import pallas as pl
from jax.experimental.pallas import tpu as pltpu

F32 = jnp.float32
BF16 = jnp.bfloat16

N_META = 16
EPS = 1e-6
ROW_TILE = 640
HG_CHUNK = 64
FF_TILE = 256
BIAS_LANES = 128
BIAS_STRIDE = 16
PAGES_PER_STEP = 16
VMEM_LIMIT = 56 * 1024 * 1024


def _cparams(n_axes):
    return pltpu.CompilerParams(dimension_semantics=("arbitrary",) * n_axes,
                                vmem_limit_bytes=VMEM_LIMIT)


def _full(shape):
    nd = len(shape)
    return pl.BlockSpec(shape, lambda *_: (0,) * nd)


def _resident(shape):
    nd = len(shape)
    return pl.BlockSpec(shape, lambda *_: (0,) * nd, pipeline_mode=pl.Buffered(1))


def _split3(x):
    hi = x.astype(BF16)
    r1 = x - hi.astype(F32)
    mid = r1.astype(BF16)
    lo = (r1 - mid.astype(F32)).astype(BF16)
    return hi, mid, lo


def _log_sigmoid(x):
    return jnp.minimum(x, 0.0) - jnp.log1p(jnp.exp(-jnp.abs(x)))


def _sigmoid(x):
    return 1.0 / (1.0 + jnp.exp(-x))


def _rms_rows(x, g):
    return x * lax.rsqrt(jnp.mean(x * x, axis=-1, keepdims=True) + EPS) * g


def _dot(a, b):
    return jnp.dot(a, b, preferred_element_type=F32)


def _dot_nt(a, b):
    return lax.dot_general(a, b, (((1,), (1,)), ((), ())), preferred_element_type=F32)


def _dot_tn(a, b):
    return lax.dot_general(a, b, (((0,), (0,)), ((), ())), preferred_element_type=F32)


def _in_proj_body(prompt, aw, hw, ah, dh, hd, x_ref, g_ref, w_ref, bfe_ref, bfp_ref, lb_ref, *rest):
    if prompt:
        (tri_ref, qT_ref, ktm_ref, kb_ref, vT_ref, ko_ref, vo_ref, alf_ref,
         hq_ref, hk_ref, hv_ref, hlf_ref, gate_ref, carry_ref) = rest
    else:
        (q_ref, ko_ref, vo_ref, alf_ref, alfe_ref,
         hq_ref, hk_ref, hv_ref, hlf_ref, gate_ref) = rest

    hb = _rms_rows(x_ref[...], g_ref[...]).astype(BF16)

    def col(c0, n):
        return _dot(hb, w_ref[:, c0:c0 + n])

    q = col(0, aw)
    k = col(aw, aw)
    v = col(2 * aw, aw)
    o = 3 * aw
    hq = col(o, hw)
    hf = col(o + hw, hw)
    hi = col(o + 2 * hw, hw)
    hg = col(o + 3 * hw, hw)
    zfe = col(o + 4 * hw, BIAS_LANES)
    zfp = col(o + 4 * hw + BIAS_LANES, BIAS_LANES)

    ko_ref[...] = k
    vo_ref[...] = v
    alfp = _log_sigmoid(zfp + bfp_ref[...])
    alf_ref[...] = alfp[:, :ah]
    alfe = _log_sigmoid(zfe + bfe_ref[...])

    lb = lb_ref[...]
    f = lb + (1.0 - lb) * _sigmoid(hf)
    hq_ref[...] = hq * (float(hd) ** -0.5)
    hk_ref[...] = 1.0 - f
    hv_ref[...] = hi
    hlf_ref[...] = jnp.log(f)
    gate_ref[...] = hg * _sigmoid(hg)

    if not prompt:
        q_ref[...] = q
        alfe_ref[...] = alfe
        return

    i = pl.program_id(1)

    @pl.when(i == 0)
    def _():
        carry_ref[...] = jnp.zeros_like(carry_ref)

    tri = tri_ref[...]
    p0, p1, p2 = _split3(alfe)
    cum = _dot(tri, p0) + _dot(tri, p1) + _dot(tri, p2) + carry_ref[...]
    tm = cum.shape[0]
    carry_ref[...] = cum[tm - 1:tm, :]
    n0, n1, n2 = _split3(-cum)
    lane = lax.broadcasted_iota(jnp.int32, cum.shape, 1) % BIAS_STRIDE
    zero = jnp.zeros_like(n0)
    kb_ref[...] = jnp.where(lane == 0, n0, jnp.where(lane == 1, n1, jnp.where(lane == 2, n2, zero)))

    qT_ref[...] = (q * (float(dh) ** -0.5)).T.astype(BF16)
    ktm_ref[...] = k.astype(BF16)
    vt = v.T
    for h in range(ah):
        vT_ref[h] = vt[h * dh:(h + 1) * dh, :].astype(BF16)


def _in_proj_prompt(x, g, w, bfe, bfp, lb, tri, L, aw, hw, ah, nh):
    B, Lp, D = x.shape
    tm = ROW_TILE
    nb = Lp // tm
    dh = aw // ah
    nc = w.shape[1]
    row = lambda width: pl.BlockSpec((None, tm, width), lambda b, i: (b, i, 0))
    out_shape = (
        jax.ShapeDtypeStruct((B, nb, aw, tm), BF16),
        jax.ShapeDtypeStruct((B, Lp, aw), BF16),
        jax.ShapeDtypeStruct((B, Lp, BIAS_LANES), BF16),
        jax.ShapeDtypeStruct((B, ah, nb, dh, tm), BF16),
        jax.ShapeDtypeStruct((B, L, aw), F32),
        jax.ShapeDtypeStruct((B, L, aw), F32),
        jax.ShapeDtypeStruct((B, L, ah), F32),
        jax.ShapeDtypeStruct((B, Lp, hw), F32),
        jax.ShapeDtypeStruct((B, Lp, hw), F32),
        jax.ShapeDtypeStruct((B, Lp, hw), F32),
        jax.ShapeDtypeStruct((B, Lp, hw), F32),
        jax.ShapeDtypeStruct((B, Lp, hw), F32),
    )
    out_specs = (
        pl.BlockSpec((None, None, aw, tm), lambda b, i: (b, i, 0, 0)),
        row(aw), row(BIAS_LANES),
        pl.BlockSpec((None, ah, None, dh, tm), lambda b, i: (b, 0, i, 0, 0)),
        row(aw), row(aw), row(ah),
        row(hw), row(hw), row(hw), row(hw), row(hw),
    )
    in_specs = [row(D), _full((1, D)), _resident((D, nc)), _full((1, BIAS_LANES)),
                _full((1, BIAS_LANES)), _full((1, hw)), _resident((tm, tm))]
    return pl.pallas_call(
        functools.partial(_in_proj_body, True, aw, hw, ah, dh, hw // nh),
        grid=(B, nb), in_specs=in_specs, out_specs=out_specs, out_shape=out_shape,
        scratch_shapes=[pltpu.VMEM((1, BIAS_LANES), F32)],
        compiler_params=_cparams(2), name="in_proj_prompt",
    )(x, g, w, bfe, bfp, lb, tri)


def _in_proj_sample(x, g, w, bfe, bfp, lb, aw, hw, ah, nh):
    R, D = x.shape
    dh = aw // ah
    nc = w.shape[1]
    out_shape = (
        jax.ShapeDtypeStruct((R, aw), F32),
        jax.ShapeDtypeStruct((R, aw), F32),
        jax.ShapeDtypeStruct((R, aw), F32),
        jax.ShapeDtypeStruct((R, ah), F32),
        jax.ShapeDtypeStruct((R, BIAS_LANES), F32),
        jax.ShapeDtypeStruct((R, hw), F32), jax.ShapeDtypeStruct((R, hw), F32),
        jax.ShapeDtypeStruct((R, hw), F32), jax.ShapeDtypeStruct((R, hw), F32),
        jax.ShapeDtypeStruct((R, hw), F32),
    )
    in_specs = [_full((R, D)), _full((1, D)), _full((D, nc)), _full((1, BIAS_LANES)),
                _full((1, BIAS_LANES)), _full((1, hw))]
    out_specs = tuple(_full(s.shape) for s in out_shape)
    return pl.pallas_call(
        functools.partial(_in_proj_body, False, aw, hw, ah, dh, hw // nh),
        grid=(1,), in_specs=in_specs, out_specs=out_specs, out_shape=out_shape,
        compiler_params=_cparams(1), name="in_proj_sample",
    )(x, g, w, bfe, bfp, lb)


def _fox_prompt_body(dh, qT_ref, k_ref, kb_ref, vT_ref, o_ref, rhs_ref):
    p = pl.program_id(1)
    qi = pl.program_id(2)
    tq = qT_ref.shape[1]
    tk = tq

    qT = qT_ref[...]
    row = lax.broadcasted_iota(jnp.int32, (2 * dh, tq), 0)
    brow = lax.broadcasted_iota(jnp.int32, (BIAS_LANES, tq), 0)
    for e in range(2):
        h = 2 * p + e
        rhs_ref[e, 0:2 * dh, :] = jnp.where(row // dh == e, qT, jnp.zeros_like(qT))
        ones = (brow // BIAS_STRIDE == h) & (brow % BIAS_STRIDE < 3)
        rhs_ref[e, 2 * dh:2 * dh + BIAS_LANES, :] = jnp.where(ones, 1.0, 0.0).astype(BF16)

    def scores(ki):
        r0 = pl.multiple_of(ki * tk, tk)
        lhs = jnp.concatenate([k_ref[pl.ds(r0, tk), :], kb_ref[pl.ds(r0, tk), :]], axis=1)
        return [_dot(lhs, rhs_ref[e]) for e in range(2)]

    def update(e, ki, s, m, l, acc):
        m_new = jnp.maximum(m, jnp.max(s, axis=0, keepdims=True))
        alpha = jnp.exp(m - m_new)
        pr = jnp.exp(s - m_new)
        l = alpha * l + jnp.sum(pr, axis=0, keepdims=True)
        acc = alpha * acc + _dot(vT_ref[e, ki], pr.astype(BF16))
        return m_new, l, acc

    def step(ki, carry):
        ss = scores(ki)
        out = []
        for e in range(2):
            out.extend(update(e, ki, ss[e], *carry[3 * e:3 * e + 3]))
        return tuple(out)

    init = []
    for e in range(2):
        init.extend([jnp.full((1, tq), -jnp.inf, F32), jnp.zeros((1, tq), F32),
                     jnp.zeros((dh, tq), F32)])
    carry = lax.fori_loop(0, qi, step, tuple(init))

    ss = scores(qi)
    kpos = lax.broadcasted_iota(jnp.int32, (tk, tq), 0)
    qpos = lax.broadcasted_iota(jnp.int32, (tk, tq), 1)
    outs = []
    for e in range(2):
        s = jnp.where(kpos <= qpos, ss[e], -jnp.inf)
        m, l, acc = update(e, qi, s, *carry[3 * e:3 * e + 3])
        outs.append(acc / l)
    o_ref[...] = jnp.concatenate(outs, axis=0).T


def _fox_prompt(qT, ktm, kb, vT):
    B, nb, aw, tq = qT.shape
    _, ah, _, dh, _ = vT.shape
    Lp = ktm.shape[1]
    npair = ah // 2
    in_specs = [
        pl.BlockSpec((None, None, 2 * dh, tq), lambda b, p, i: (b, i, p, 0)),
        pl.BlockSpec((None, Lp, 2 * dh), lambda b, p, i: (b, 0, p)),
        pl.BlockSpec((None, Lp, BIAS_LANES), lambda b, p, i: (b, 0, 0)),
        pl.BlockSpec((None, 2, nb, dh, tq), lambda b, p, i: (b, p, 0, 0, 0)),
    ]
    return pl.pallas_call(
        functools.partial(_fox_prompt_body, dh),
        grid=(B, npair, nb), in_specs=in_specs,
        out_specs=pl.BlockSpec((None, tq, 2 * dh), lambda b, p, i: (b, i, p)),
        out_shape=jax.ShapeDtypeStruct((B, Lp, aw), F32),
        scratch_shapes=[pltpu.VMEM((2, 2 * dh + BIAS_LANES, tq), BF16)],
        compiler_params=_cparams(3), name="fox_prompt",
    )(qT, ktm, kb, vT)


def _hgrn_tables():
    C = HG_CHUNK
    levels = []
    m = C // 2
    while m >= 1:
        levels.append(m)
        m //= 2
    M = np.zeros((2 + len(levels), C, C), np.float32)
    for r in range(C):
        M[0, r, :r + 1] = 1.0
        M[1, r, r + 1:] = 1.0
    LV = -np.ones((C, C), np.int32)
    for li, m in enumerate(levels):
        for r in range(C):
            g = (r // (2 * m)) * 2 * m
            if r - g >= m:
                M[2 + li, r, g + m:r + 1] = 1.0
            else:
                M[2 + li, r, r + 1:g + m] = 1.0
        for t in range(C):
            for s in range(t):
                if t // (2 * m) == s // (2 * m) and t % (2 * m) >= m and s % (2 * m) < m:
                    LV[t, s] = li
    for t in range(C):
        LV[t, t] = len(levels)
    return M.reshape(-1, C), LV, levels


def _hgrn_prompt_body(L, nh, levels, q_ref, k_ref, v_ref, lf_ref, mall_ref, lv_ref,
                      o_ref, s_ref, st_ref):
    i = pl.program_id(1)
    nblk = pl.num_programs(1)
    tm = q_ref.shape[0]
    C = HG_CHUNK
    hd = q_ref.shape[1] // nh
    ng = 2 + len(levels)

    @pl.when(i == 0)
    def _():
        st_ref[...] = jnp.zeros_like(st_ref)

    mall = mall_ref[...]
    lv = lv_ref[...]
    rowi = lax.broadcasted_iota(jnp.int32, (C, hd), 0)

    def chunk(c, carry):
        r0 = pl.multiple_of(c * C, C)
        valid = (i * tm + r0 + rowi) < L
        new = []
        for h in range(nh):
            sl = slice(h * hd, (h + 1) * hd)
            st = carry[h]
            q = q_ref[pl.ds(r0, C), sl]
            k = k_ref[pl.ds(r0, C), sl]
            v = jnp.where(valid, v_ref[pl.ds(r0, C), sl], 0.0)
            lf = jnp.where(valid, lf_ref[pl.ds(r0, C), sl], 0.0)
            l0, l1, l2 = _split3(lf)
            ex = jnp.exp(_dot(mall, l0) + _dot(mall, l1) + _dot(mall, l2))
            qh = (q * ex[0:C]).astype(BF16)
            kh = (k * ex[C:2 * C]).astype(BF16)
            vb = v.astype(BF16)
            a = jnp.where(lv == len(levels), _dot_nt(q.astype(BF16), k.astype(BF16)), 0.0)
            for li, m in enumerate(levels):
                w = (jnp.where((rowi & m) != 0, q, k) * ex[(2 + li) * C:(3 + li) * C]).astype(BF16)
                a = a + jnp.where(lv == li, _dot_nt(w, w), 0.0)
            o = _dot_nt(qh, st.astype(BF16)) + _dot(a.astype(BF16), vb)
            o_ref[pl.ds(r0, C), sl] = o
            decay = ex[C - 1:C, :]
            new.append(st * decay + _dot_tn(vb, kh))
        return tuple(new)

    carry = lax.fori_loop(0, tm // C, chunk, tuple(st_ref[h] for h in range(nh)))
    for h in range(nh):
        st_ref[h] = carry[h]

    @pl.when(i == nblk - 1)
    def _():
        for h in range(nh):
            s_ref[h] = carry[h].T


def _hgrn_prompt(hq, hk, hv, hlf, mall, lv, levels, L, nh):
    B, Lp, hw = hq.shape
    tm = ROW_TILE
    nb = Lp // tm
    hd = hw // nh
    row = pl.BlockSpec((None, tm, hw), lambda b, i: (b, i, 0))
    return pl.pallas_call(
        functools.partial(_hgrn_prompt_body, L, nh, levels),
        grid=(B, nb),
        in_specs=[row, row, row, row, _full(mall.shape), _full(lv.shape)],
        out_specs=(row, pl.BlockSpec((None, nh, hd, hd), lambda b, i: (b, 0, 0, 0))),
        out_shape=(jax.ShapeDtypeStruct((B, Lp, hw), F32),
                   jax.ShapeDtypeStruct((B, nh, hd, hd), F32)),
        scratch_shapes=[pltpu.VMEM((nh, hd, hd), F32)],
        compiler_params=_cparams(2), name="hgrn_prompt",
    )(hq, hk, hv, hlf, mall, lv)


def _merge_body(nh, x_ref, att_ref, ho_ref, gate_ref, ga_ref, gh_ref, gf_ref, w_ref,
                xo_ref, h2_ref):
    att = att_ref[...]
    a = _rms_rows(att, ga_ref[...])
    ho = ho_ref[...]
    hd = ho.shape[1] // nh
    gh = gh_ref[...]
    parts = [a]
    for h in range(nh):
        sl = slice(h * hd, (h + 1) * hd)
        parts.append(_rms_rows(ho[:, sl], gh[:, sl]) * gate_ref[:, sl])
    m = jnp.concatenate(parts, axis=1).astype(BF16)
    xn = x_ref[...] + _dot(m, w_ref[...])
    xo_ref[...] = xn
    h2_ref[...] = _rms_rows(xn, gf_ref[...]).astype(BF16)


def _merge(x, att, ho, gate, ga, gh, gf, w, nh, tm):
    R, D = x.shape
    aw = att.shape[1]
    hw = ho.shape[1]
    row = lambda width: pl.BlockSpec((tm, width), lambda i: (i, 0))
    return pl.pallas_call(
        functools.partial(_merge_body, nh),
        grid=(R // tm,),
        in_specs=[row(D), row(aw), row(hw), row(hw), _full((1, aw)), _full((1, hw)),
                  _full((1, D)), _resident(w.shape)],
        out_specs=(row(D), row(D)),
        out_shape=(jax.ShapeDtypeStruct((R, D), F32), jax.ShapeDtypeStruct((R, D), BF16)),
        compiler_params=_cparams(1), name="merge_out",
    )(x, att, ho, gate, ga, gh, gf, w)


def _convffn_body(seq_len, state_rows, x_ref, h2_ref, wup_ref, cw_ref, cb_ref, wdn_ref, *rest):
    if seq_len is None:
        xo_ref, cs_ref, halo_ref, buf_ref, acc_ref = rest
    else:
        p1_ref, p2_ref, xo_ref, a_ref, halo_ref, buf_ref, acc_ref = rest
    i = pl.program_id(1)
    tm = h2_ref.shape[0]
    dff = wdn_ref.shape[0]
    tn = FF_TILE
    h2 = h2_ref[...]

    @pl.when(i == 0)
    def _():
        halo_ref[...] = jnp.zeros_like(halo_ref)

    if seq_len is not None:
        tmod = lax.broadcasted_iota(jnp.int32, (tm, tn), 0) % seq_len

    for c in range(dff // tn):
        cs = slice(c * tn, (c + 1) * tn)
        a = _dot(h2, wup_ref[:, c * tn:(c + 1) * tn])
        b = _dot(h2, wup_ref[:, dff + c * tn:dff + (c + 1) * tn])
        buf_ref[0:8, :] = halo_ref[:, cs]
        buf_ref[8:tm + 8, :] = a
        halo_ref[:, cs] = a[tm - 8:tm, :]
        a1 = buf_ref[7:tm + 7, :]
        a2 = buf_ref[6:tm + 6, :]
        if seq_len is None:
            lo, blk = state_rows
            @pl.when(i == blk)
            def _():
                cs_ref[:, cs] = a[lo:lo + 2, :]
        else:
            a1 = jnp.where(tmod >= 1, a1, p1_ref[:, cs])
            a2 = jnp.where(tmod >= 2, a2, p2_ref[:, cs])
            a_ref[:, cs] = a
        cv = cb_ref[:, cs] + cw_ref[0:1, cs] * a2 + cw_ref[1:2, cs] * a1 + cw_ref[2:3, cs] * a
        gt = (cv * _sigmoid(cv) * b).astype(BF16)
        y = _dot(gt, wdn_ref[c * tn:(c + 1) * tn, :])
        if c == 0:
            acc_ref[...] = y
        else:
            acc_ref[...] += y
    xo_ref[...] = x_ref[...] + acc_ref[...]


def _convffn_prompt(x, h2, wup, cw, cb, wdn, L):
    B, Lp, D = x.shape
    tm = ROW_TILE
    nb = Lp // tm
    dff = wdn.shape[0]
    row = pl.BlockSpec((None, tm, D), lambda b, i: (b, i, 0))
    state_rows = ((L - 2) % tm, (L - 2) // tm)
    return pl.pallas_call(
        functools.partial(_convffn_body, None, state_rows),
        grid=(B, nb),
        in_specs=[row, row, _resident(wup.shape), _full(cw.shape), _full(cb.shape),
                  _resident(wdn.shape)],
        out_specs=(row, pl.BlockSpec((None, 2, dff), lambda b, i: (b, 0, 0))),
        out_shape=(jax.ShapeDtypeStruct((B, Lp, D), F32), jax.ShapeDtypeStruct((B, 2, dff), F32)),
        scratch_shapes=[pltpu.VMEM((8, dff), F32), pltpu.VMEM((tm + 8, FF_TILE), F32),
                        pltpu.VMEM((tm, D), F32)],
        compiler_params=_cparams(2), name="convffn_prompt",
    )(x, h2, wup, cw, cb, wdn)


def _convffn_sample(x, h2, wup, cw, cb, wdn, p1, p2, seq_len):
    R, D = x.shape
    dff = wdn.shape[0]
    full2 = lambda shape: pl.BlockSpec(shape, lambda b, i: (0, 0))
    return pl.pallas_call(
        functools.partial(_convffn_body, seq_len, None),
        grid=(1, 1),
        in_specs=[full2((R, D)), full2((R, D)), full2(wup.shape), full2(cw.shape),
                  full2(cb.shape), full2(wdn.shape), full2((R, dff)), full2((R, dff))],
        out_specs=(full2((R, D)), full2((R, dff))),
        out_shape=(jax.ShapeDtypeStruct((R, D), F32), jax.ShapeDtypeStruct((R, dff), F32)),
        scratch_shapes=[pltpu.VMEM((8, dff), F32), pltpu.VMEM((R + 8, FF_TILE), F32),
                        pltpu.VMEM((R, D), F32)],
        compiler_params=_cparams(2), name="convffn_sample",
    )(x, h2, wup, cw, cb, wdn, p1, p2)


def _fox_sample_body(npp, ah, dh, T, pt_ref, q_ref, kn_ref, vn_ref, alfe_ref, uf_ref, wx_ref,
                     *rest):
    k_refs = rest[0:npp]
    v_refs = rest[npp:2 * npp]
    lf_refs = rest[2 * npp:3 * npp]
    o_ref, qbd_ref, m_ref, l_ref, acc_ref, tail_ref = rest[3 * npp:]
    g = pl.program_id(1)
    ng = pl.num_programs(1)
    aw = ah * dh
    page = k_refs[0].shape[0]
    R = T * ah

    lane = lax.broadcasted_iota(jnp.int32, (ah, aw), 1)
    head = lax.broadcasted_iota(jnp.int32, (ah, aw), 0)
    hmask = lane // dh == head

    @pl.when(g == 0)
    def _():
        q = q_ref[...] * (float(dh) ** -0.5)
        blocks = [jnp.where(hmask, jnp.broadcast_to(q[t:t + 1, :], (ah, aw)), 0.0)
                  for t in range(T)]
        qbd = jnp.concatenate(blocks, axis=0).astype(BF16)
        qbd_ref[...] = qbd
        rows = [alfe_ref[0:1, :]]
        for t in range(1, T):
            rows.append(rows[-1] + alfe_ref[t:t + 1, :])
        rows.extend([jnp.zeros_like(rows[0])] * (8 - T))
        cn = jnp.concatenate(rows, axis=0)
        n0, n1, n2 = _split3(-cn)
        bl = lax.broadcasted_iota(jnp.int32, cn.shape, 1) % BIAS_STRIDE
        zero = jnp.zeros_like(n0)
        kbn = jnp.where(bl == 0, n0, jnp.where(bl == 1, n1, jnp.where(bl == 2, n2, zero)))
        kn = jnp.concatenate([kn_ref[...], jnp.zeros((8 - T, aw), F32)], axis=0).astype(BF16)
        rl = lax.broadcasted_iota(jnp.int32, (R, BIAS_LANES), 1)
        rh = lax.broadcasted_iota(jnp.int32, (R, BIAS_LANES), 0) % ah
        qb = jnp.where((rl // BIAS_STRIDE == rh) & (rl % BIAS_STRIDE < 3), 1.0, 0.0).astype(BF16)
        s = _dot_nt(qbd, kn) + _dot_nt(qb, kbn)
        col = lax.broadcasted_iota(jnp.int32, s.shape, 1)
        rt = lax.broadcasted_iota(jnp.int32, s.shape, 0) // ah
        s = jnp.where((col <= rt) & (col < T), s, -jnp.inf)
        m = jnp.max(s, axis=1, keepdims=True)
        pr = jnp.exp(s - m)
        m_ref[...] = m
        l_ref[...] = jnp.sum(pr, axis=1, keepdims=True)
        vn = jnp.concatenate([vn_ref[...], jnp.zeros((8 - T, aw), F32)], axis=0).astype(BF16)
        acc_ref[...] = _dot(pr.astype(BF16), vn)
        tail_ref[...] = jnp.zeros_like(tail_ref)

    lft = jnp.concatenate([r[...] for r in lf_refs], axis=0)
    p0, p1, p2 = _split3(lft)
    uf = uf_ref[...]
    wx = wx_ref[...]
    within = _dot(p0, uf) + _dot(p1, uf) + _dot(p2, uf)
    later = jnp.sum(_dot(wx, p0) + _dot(wx, p1) + _dot(wx, p2), axis=1, keepdims=True)
    tail = tail_ref[...]
    bias = within + later + jnp.concatenate([tail] * npp, axis=0)
    tot = jnp.sum(lft, axis=1, keepdims=True)
    step_tot = tot[0:ah]
    for j in range(1, npp):
        step_tot = step_tot + tot[j * ah:(j + 1) * ah]
    tail_ref[...] = tail + step_tot

    qbd = qbd_ref[...]
    s_parts = []
    for j in range(npp):
        kj = k_refs[j][...].astype(BF16)
        bj = bias[j * ah:(j + 1) * ah, :]
        s_parts.append(_dot_nt(qbd, kj) + jnp.concatenate([bj] * T, axis=0))
    s = jnp.concatenate(s_parts, axis=1)
    m_old = m_ref[...]
    m_new = jnp.maximum(m_old, jnp.max(s, axis=1, keepdims=True))
    alpha = jnp.exp(m_old - m_new)
    pr = jnp.exp(s - m_new)
    l_ref[...] = alpha * l_ref[...] + jnp.sum(pr, axis=1, keepdims=True)
    m_ref[...] = m_new
    prb = pr.astype(BF16)
    acc = alpha * acc_ref[...]
    for j in range(npp):
        acc = acc + _dot(prb[:, j * page:(j + 1) * page], v_refs[j][...].astype(BF16))
    acc_ref[...] = acc

    @pl.when(g == ng - 1)
    def _():
        out = acc / l_ref[...]
        rows = []
        for t in range(T):
            blk = jnp.where(hmask, out[t * ah:(t + 1) * ah, :], 0.0)
            rows.append(jnp.sum(blk, axis=0, keepdims=True))
        o_ref[...] = jnp.concatenate(rows, axis=0)


def _fox_sample(layer, page_table, q, kn, vn, alfe, cache_k, cache_v, cache_lft, uf, wx, ah):
    Bd, T, aw = q.shape
    dh = aw // ah
    n_pages = page_table.shape[1]
    page = cache_k.shape[2]
    npp = min(PAGES_PER_STEP, n_pages)
    ng = n_pages // npp
    R = T * ah

    def page_map(r):
        return lambda b, g, pt: (layer, pt[b, n_pages - (g + 1) * npp + r], 0, 0)

    seq = lambda width: pl.BlockSpec((None, T, width), lambda b, g, pt: (b, 0, 0))
    const = lambda shape: pl.BlockSpec(shape, lambda b, g, pt: (0,) * len(shape))
    in_specs = [seq(aw), seq(aw), seq(aw), seq(BIAS_LANES), const(uf.shape), const(wx.shape)]
    in_specs += [pl.BlockSpec((None, None, page, aw), page_map(r)) for r in range(npp)]
    in_specs += [pl.BlockSpec((None, None, page, aw), page_map(r)) for r in range(npp)]
    in_specs += [pl.BlockSpec((None, None, ah, page), page_map(r)) for r in range(npp)]
    grid_spec = pltpu.PrefetchScalarGridSpec(
        num_scalar_prefetch=1, grid=(Bd, ng), in_specs=in_specs,
        out_specs=pl.BlockSpec((None, T, aw), lambda b, g, pt: (b, 0, 0)),
        scratch_shapes=[pltpu.VMEM((R, aw), BF16), pltpu.VMEM((R, 1), F32),
                        pltpu.VMEM((R, 1), F32), pltpu.VMEM((R, aw), F32),
                        pltpu.VMEM((ah, 1), F32)])
    return pl.pallas_call(
        functools.partial(_fox_sample_body, npp, ah, dh, T),
        grid_spec=grid_spec, out_shape=jax.ShapeDtypeStruct((Bd, T, aw), F32),
        compiler_params=_cparams(2), name="fox_sample",
    )(page_table, q, kn, vn, alfe, uf, wx, *([cache_k] * npp), *([cache_v] * npp),
      *([cache_lft] * npp))


def _hgrn_sample_body(nh, T, q_ref, k_ref, v_ref, lf_ref, s0_ref, o_ref, s1_ref):
    hd = s0_ref.shape[-1]
    outs = [[] for _ in range(T)]
    for h in range(nh):
        sl = slice(h * hd, (h + 1) * hd)
        q = [q_ref[t:t + 1, sl] for t in range(T)]
        k = [k_ref[t:t + 1, sl] for t in range(T)]
        v = [v_ref[t:t + 1, sl] for t in range(T)]
        cum = [lf_ref[0:1, sl]]
        for t in range(1, T):
            cum.append(cum[-1] + lf_ref[t:t + 1, sl])
        s0 = s0_ref[h]
        pad = [jnp.zeros((1, hd), F32)] * (8 - T)
        qh = jnp.concatenate([q[t] * jnp.exp(cum[t]) for t in range(T)] + pad, axis=0)
        o_inter = _dot(qh.astype(BF16), s0.astype(BF16))
        for t in range(T):
            o = o_inter[t:t + 1, :]
            for s in range(t + 1):
                a = jnp.sum(q[t] * k[s] * jnp.exp(cum[t] - cum[s]), axis=1, keepdims=True)
                o = o + a * v[s]
            outs[t].append(o)
        kh = jnp.concatenate([k[s] * jnp.exp(cum[T - 1] - cum[s]) for s in range(T)] + pad, axis=0)
        vv = jnp.concatenate(v + pad, axis=0)
        decay_col = jnp.broadcast_to(jnp.exp(cum[T - 1]), (hd, hd)).T
        s1_ref[h] = decay_col * s0 + _dot_tn(kh.astype(BF16), vv.astype(BF16))
    for t in range(T):
        o_ref[t:t + 1, :] = jnp.concatenate(outs[t], axis=1)


def _hgrn_sample(hq, hk, hv, hlf, s0, nh):
    Bd, T, hw = hq.shape
    hd = hw // nh
    seq = pl.BlockSpec((None, T, hw), lambda b: (b, 0, 0))
    st = pl.BlockSpec((None, nh, hd, hd), lambda b: (b, 0, 0, 0))
    return pl.pallas_call(
        functools.partial(_hgrn_sample_body, nh, T),
        grid=(Bd,), in_specs=[seq, seq, seq, seq, st], out_specs=(seq, st),
        out_shape=(jax.ShapeDtypeStruct((Bd, T, hw), F32),
                   jax.ShapeDtypeStruct((Bd, nh, hd, hd), F32)),
        compiler_params=_cparams(1), name="hgrn_sample",
    )(hq, hk, hv, hlf, s0)


def _final_norm_body(x_ref, g_ref, o_ref):
    o_ref[...] = _rms_rows(x_ref[...], g_ref[...])


def _final_norm_shifted_body(x_ref, g_ref, o_ref):
    o_ref[...] = _rms_rows(x_ref[0], g_ref[...])


def _final_norm_prompt(x, g, seq, tm):
    B, _, D = x.shape
    return pl.pallas_call(
        _final_norm_shifted_body, grid=(B, seq // tm),
        in_specs=[pl.BlockSpec((pl.Element(1), pl.Element(tm), pl.Element(D)),
                               lambda b, i: (b, pl.multiple_of(N_META + i * tm, 8), 0)),
                  _full((1, D))],
        out_specs=pl.BlockSpec((None, tm, D), lambda b, i: (b, i, 0)),
        out_shape=jax.ShapeDtypeStruct((B, seq, D), F32),
        compiler_params=_cparams(2), name="final_norm_prompt",
    )(x, g)


def _final_norm_sample(x, g):
    return pl.pallas_call(
        _final_norm_body, grid=(1,), in_specs=[_full(x.shape), _full(g.shape)],
        out_specs=_full(x.shape), out_shape=jax.ShapeDtypeStruct(x.shape, F32),
        compiler_params=_cparams(1), name="final_norm_sample",
    )(x, g)


def _expand_heads(vec, ah):
    rep = jnp.repeat(vec.astype(F32), BIAS_STRIDE)
    return jnp.pad(rep, (0, BIAS_LANES - ah * BIAS_STRIDE))[None, :]


def kernel(x_prompt, x_sample, cache_k, cache_v, cache_logf, state_hgrn, state_conv, page_table,
           meta_tokens, g_mix, w_in, b_f, hg_lb, g_att_out, g_hg_out, w_out, g_ffn, w_up,
           conv_w, conv_b, w_down, g_final):
    B, seq, D = x_prompt.shape
    Bd, T, _ = x_sample.shape
    depth = w_in.shape[0]
    ah = b_f.shape[1]
    aw = cache_k.shape[3] * cache_k.shape[4]
    hw = hg_lb.shape[1]
    nh = state_hgrn.shape[2]
    dff = w_down.shape[1]
    n_pool, page = cache_k.shape[1], cache_k.shape[2]
    L = N_META + seq
    tm = ROW_TILE
    Lp = -(-L // tm) * tm
    assert ah * BIAS_STRIDE <= BIAS_LANES and ah % 2 == 0 and dff % FF_TILE == 0
    assert page_table.shape[1] % min(PAGES_PER_STEP, page_table.shape[1]) == 0 and T <= 8

    sm = jax.nn.softmax(hg_lb.astype(F32), axis=0)
    lb_all = jnp.cumsum(sm, axis=0) - sm[0:1]

    tri = jnp.asarray(np.tril(np.ones((tm, tm), np.float32)), BF16)
    mall_np, lv_np, levels = _hgrn_tables()
    mall = jnp.asarray(mall_np, BF16)
    lv = jnp.asarray(lv_np, jnp.int32)
    npp = min(PAGES_PER_STEP, page_table.shape[1])
    uf = jnp.asarray(np.tril(np.ones((page, page), np.float32), -1), BF16)
    jj = np.arange(npp * ah)
    wx = jnp.asarray(((jj[None, :] % ah == jj[:, None] % ah) &
                      (jj[None, :] // ah > jj[:, None] // ah)).astype(np.float32), BF16)

    cache_k4 = cache_k.reshape(depth, n_pool, page, aw)
    cache_v4 = cache_v.reshape(depth, n_pool, page, aw)
    cache_lft = jnp.swapaxes(cache_logf, 2, 3)

    xp = jnp.concatenate([jnp.broadcast_to(meta_tokens[None].astype(F32), (B, N_META, D)),
                          x_prompt, jnp.zeros((B, Lp - L, D), F32)], axis=1)
    xs = x_sample.reshape(Bd * T, D)

    outs = [[] for _ in range(10)]
    for l in range(depth):
        wi = w_in[l]
        o = 3 * aw + ah
        zf_cols = wi[:, 3 * aw:o]
        w_arr = jnp.concatenate(
            [wi[:, :3 * aw], wi[:, o:o + 4 * hw],
             jnp.pad(jnp.repeat(zf_cols, BIAS_STRIDE, axis=1),
                     ((0, 0), (0, BIAS_LANES - ah * BIAS_STRIDE))),
             jnp.pad(zf_cols, ((0, 0), (0, BIAS_LANES - ah)))], axis=1).astype(BF16)
        bfe = _expand_heads(b_f[l], ah)
        bfp = jnp.pad(b_f[l].astype(F32), (0, BIAS_LANES - ah))[None, :]
        lb = lb_all[l][None, :]
        g1 = g_mix[l][None, :].astype(F32)
        ga = g_att_out[l][None, :].astype(F32)
        gh = g_hg_out[l][None, :].astype(F32)
        gf = g_ffn[l][None, :].astype(F32)
        wo = w_out[l].astype(BF16)
        wu = w_up[l].astype(BF16)
        wd = w_down[l].astype(BF16)
        cw = conv_w[l].astype(F32)
        cb = conv_b[l][None, :].astype(F32)

        (qT, ktm, kb, vT, k_new, v_new, alf, hq, hk, hv, hlf, gate) = _in_proj_prompt(
            xp, g1, w_arr, bfe, bfp, lb, tri, L, aw, hw, ah, nh)
        att = _fox_prompt(qT, ktm, kb, vT)
        ho, s_p = _hgrn_prompt(hq, hk, hv, hlf, mall, lv, levels, L, nh)
        xp2, h2 = _merge(xp.reshape(B * Lp, D), att.reshape(B * Lp, aw), ho.reshape(B * Lp, hw),
                         gate.reshape(B * Lp, hw), ga, gh, gf, wo, nh, tm)
        xp, c_p = _convffn_prompt(xp2.reshape(B, Lp, D), h2.reshape(B, Lp, D), wu, cw, cb, wd, L)
        outs[0].append(k_new.reshape(B, L, ah, aw // ah))
        outs[1].append(v_new.reshape(B, L, ah, aw // ah))
        outs[2].append(alf)
        outs[3].append(s_p)
        outs[4].append(c_p)

        (sq, sk, sv, salf, salfe, shq, shk, shv, shlf, sgate) = _in_proj_sample(
            xs, g1, w_arr, bfe, bfp, lb, aw, hw, ah, nh)
        att_s = _fox_sample(l, page_table, sq.reshape(Bd, T, aw), sk.reshape(Bd, T, aw),
                            sv.reshape(Bd, T, aw), salfe.reshape(Bd, T, BIAS_LANES),
                            cache_k4, cache_v4, cache_lft, uf, wx, ah)
        ho_s, s_s = _hgrn_sample(shq.reshape(Bd, T, hw), shk.reshape(Bd, T, hw),
                                 shv.reshape(Bd, T, hw), shlf.reshape(Bd, T, hw),
                                 state_hgrn[l].astype(F32), nh)
        xs2, h2s = _merge(xs, att_s.reshape(Bd * T, aw), ho_s.reshape(Bd * T, hw), sgate,
                          ga, gh, gf, wo, nh, Bd * T)
        prev = state_conv[l].astype(F32)
        zeros = jnp.zeros((Bd, T - 1, dff), F32)
        p1 = jnp.concatenate([prev[:, 1:2], zeros], axis=1).reshape(Bd * T, dff)
        p2 = jnp.concatenate([prev, zeros[:, :T - 2]], axis=1).reshape(Bd * T, dff)
        xs, a_s = _convffn_sample(xs2, h2s, wu, cw, cb, wd, p1, p2, T)
        outs[5].append(sk.reshape(Bd, T, ah, aw // ah))
        outs[6].append(sv.reshape(Bd, T, ah, aw // ah))
        outs[7].append(salf.reshape(Bd, T, ah))
        outs[8].append(s_s)
        outs[9].append(a_s.reshape(Bd, T, dff)[:, T - 2:])

    gfin = g_final[None, :].astype(F32)
    y_prompt = _final_norm_prompt(xp, gfin, seq, 512 if seq % 512 == 0 else seq)
    y_sample = _final_norm_sample(xs, gfin).reshape(Bd, T, D)
    return (y_prompt, y_sample) + tuple(jnp.stack(o) for o in outs)
```

```python
import functools

import numpy as np
import jax
import jax.numpy as jnp
from jax import lax
from jax.experimental import pallas as pl
from jax.experimental.pallas import tpu as pltpu

F32 = jnp.float32
BF16 = jnp.bfloat16

N_META = 16
EPS = 1e-6
LOG2E = 1.4426950408889634
ROW_TILE = 640
HG_CHUNK = 64
FF_TILE = 256
BIAS_LANES = 128
BIAS_STRIDE = 16
PAGES_PER_STEP = 8
NEW_LANES = 128
VMEM_LIMIT = 56 * 1024 * 1024


def _cparams(n_axes):
    return pltpu.CompilerParams(dimension_semantics=("arbitrary",) * n_axes,
                                vmem_limit_bytes=VMEM_LIMIT)


def _full(shape):
    nd = len(shape)
    return pl.BlockSpec(shape, lambda *_: (0,) * nd)


def _resident(shape):
    nd = len(shape)
    return pl.BlockSpec(shape, lambda *_: (0,) * nd, pipeline_mode=pl.Buffered(1))


def _split3(x):
    hi = x.astype(BF16)
    r1 = x - hi.astype(F32)
    mid = r1.astype(BF16)
    lo = (r1 - mid.astype(F32)).astype(BF16)
    return hi, mid, lo


def _log_sigmoid(x):
    return jnp.minimum(x, 0.0) - jnp.log1p(jnp.exp(-jnp.abs(x)))


def _sigmoid(x):
    return 1.0 / (1.0 + jnp.exp(-x))


def _rms_rows(x, g):
    return x * lax.rsqrt(jnp.mean(x * x, axis=-1, keepdims=True) + EPS) * g


def _dot(a, b):
    return jnp.dot(a, b, preferred_element_type=F32)


def _dot_nt(a, b):
    return lax.dot_general(a, b, (((1,), (1,)), ((), ())), preferred_element_type=F32)


def _dot_tn(a, b):
    return lax.dot_general(a, b, (((0,), (0,)), ((), ())), preferred_element_type=F32)


def _in_proj_body(prompt, aw, hw, ah, dh, hd, x_ref, g_ref, w_ref, bfe_ref, bfp_ref, lb_ref, *rest):
    if prompt:
        (tri_ref, qT_ref, ktm_ref, kb_ref, vT_ref, ko_ref, vo_ref, alf_ref,
         hq_ref, hk_ref, hv_ref, hlf_ref, gate_ref, carry_ref) = rest
    else:
        (q_ref, ko_ref, vo_ref, alf_ref,
         hq_ref, hk_ref, hv_ref, hlf_ref, gate_ref) = rest

    hb = _rms_rows(x_ref[...], g_ref[...]).astype(BF16)

    def col(c0, n):
        return _dot(hb, w_ref[:, c0:c0 + n])

    q = col(0, aw)
    k = col(aw, aw)
    v = col(2 * aw, aw)
    o = 3 * aw
    hq = col(o, hw)
    hf = col(o + hw, hw)
    hi = col(o + 2 * hw, hw)
    hg = col(o + 3 * hw, hw)
    zfp = col(o + 4 * hw + BIAS_LANES, BIAS_LANES)

    ko_ref[...] = k
    vo_ref[...] = v
    alfp = _log_sigmoid(zfp + bfp_ref[...])
    alf_ref[...] = alfp[:, :ah]

    lb = lb_ref[...]
    f = lb + (1.0 - lb) * _sigmoid(hf)
    hq_ref[...] = hq * (float(hd) ** -0.5)
    hk_ref[...] = 1.0 - f
    hv_ref[...] = hi
    hlf_ref[...] = jnp.log(f)
    gate_ref[...] = hg * _sigmoid(hg)

    if not prompt:
        q_ref[...] = q
        return

    alfe = _log_sigmoid(col(o + 4 * hw, BIAS_LANES) + bfe_ref[...])
    i = pl.program_id(1)

    @pl.when(i == 0)
    def _():
        carry_ref[...] = jnp.zeros_like(carry_ref)

    tri = tri_ref[...]
    p0, p1, p2 = _split3(alfe)
    cum = _dot(tri, p0) + _dot(tri, p1) + _dot(tri, p2) + carry_ref[...]
    tm = cum.shape[0]
    carry_ref[...] = cum[tm - 1:tm, :]
    n0, n1, n2 = _split3(cum * (-LOG2E))
    lane = lax.broadcasted_iota(jnp.int32, cum.shape, 1) % BIAS_STRIDE
    zero = jnp.zeros_like(n0)
    kb_ref[...] = jnp.where(lane == 0, n0, jnp.where(lane == 1, n1, jnp.where(lane == 2, n2, zero)))

    qT_ref[...] = (q * (float(dh) ** -0.5 * LOG2E)).T.astype(BF16)
    ktm_ref[...] = k.astype(BF16)
    vt = v.T
    for h in range(ah):
        vT_ref[h] = vt[h * dh:(h + 1) * dh, :].astype(BF16)


def _in_proj_prompt(x, g, w, bfe, bfp, lb, tri, L, aw, hw, ah, nh):
    B, Lp, D = x.shape
    tm = ROW_TILE
    nb = Lp // tm
    dh = aw // ah
    nc = w.shape[1]
    row = lambda width: pl.BlockSpec((None, tm, width), lambda b, i: (b, i, 0))
    out_shape = (
        jax.ShapeDtypeStruct((B, nb, aw, tm), BF16),
        jax.ShapeDtypeStruct((B, Lp, aw), BF16),
        jax.ShapeDtypeStruct((B, Lp, BIAS_LANES), BF16),
        jax.ShapeDtypeStruct((B, ah, nb, dh, tm), BF16),
        jax.ShapeDtypeStruct((B, L, aw), F32),
        jax.ShapeDtypeStruct((B, L, aw), F32),
        jax.ShapeDtypeStruct((B, L, ah), F32),
        jax.ShapeDtypeStruct((B, Lp, hw), F32),
        jax.ShapeDtypeStruct((B, Lp, hw), F32),
        jax.ShapeDtypeStruct((B, Lp, hw), F32),
        jax.ShapeDtypeStruct((B, Lp, hw), F32),
        jax.ShapeDtypeStruct((B, Lp, hw), F32),
    )
    out_specs = (
        pl.BlockSpec((None, None, aw, tm), lambda b, i: (b, i, 0, 0)),
        row(aw), row(BIAS_LANES),
        pl.BlockSpec((None, ah, None, dh, tm), lambda b, i: (b, 0, i, 0, 0)),
        row(aw), row(aw), row(ah),
        row(hw), row(hw), row(hw), row(hw), row(hw),
    )
    in_specs = [row(D), _full((1, D)), _resident((D, nc)), _full((1, BIAS_LANES)),
                _full((1, BIAS_LANES)), _full((1, hw)), _resident((tm, tm))]
    return pl.pallas_call(
        functools.partial(_in_proj_body, True, aw, hw, ah, dh, hw // nh),
        grid=(B, nb), in_specs=in_specs, out_specs=out_specs, out_shape=out_shape,
        scratch_shapes=[pltpu.VMEM((1, BIAS_LANES), F32)],
        compiler_params=_cparams(2), name="in_proj_prompt",
    )(x, g, w, bfe, bfp, lb, tri)


def _in_proj_sample(x, g, w, bfe, bfp, lb, aw, hw, ah, nh):
    R, D = x.shape
    dh = aw // ah
    nc = w.shape[1]
    out_shape = (
        jax.ShapeDtypeStruct((R, aw), F32),
        jax.ShapeDtypeStruct((R, aw), F32),
        jax.ShapeDtypeStruct((R, aw), F32),
        jax.ShapeDtypeStruct((R, ah), F32),
        jax.ShapeDtypeStruct((R, hw), F32), jax.ShapeDtypeStruct((R, hw), F32),
        jax.ShapeDtypeStruct((R, hw), F32), jax.ShapeDtypeStruct((R, hw), F32),
        jax.ShapeDtypeStruct((R, hw), F32),
    )
    in_specs = [_full((R, D)), _full((1, D)), _full((D, nc)), _full((1, BIAS_LANES)),
                _full((1, BIAS_LANES)), _full((1, hw))]
    out_specs = tuple(_full(s.shape) for s in out_shape)
    return pl.pallas_call(
        functools.partial(_in_proj_body, False, aw, hw, ah, dh, hw // nh),
        grid=(1,), in_specs=in_specs, out_specs=out_specs, out_shape=out_shape,
        compiler_params=_cparams(1), name="in_proj_sample",
    )(x, g, w, bfe, bfp, lb)


def _fox_prompt_body(dh, qT_ref, k_ref, kb_ref, vT_ref, o_ref, rhs_ref):
    p = pl.program_id(1)
    qi = pl.program_id(2)
    tq = qT_ref.shape[1]
    tk = tq

    qT = qT_ref[...]
    row = lax.broadcasted_iota(jnp.int32, (2 * dh, tq), 0)
    brow = lax.broadcasted_iota(jnp.int32, (BIAS_LANES, tq), 0)
    for e in range(2):
        h = 2 * p + e
        rhs_ref[e, 0:2 * dh, :] = jnp.where(row // dh == e, qT, jnp.zeros_like(qT))
        ones = (brow // BIAS_STRIDE == h) & (brow % BIAS_STRIDE < 3)
        rhs_ref[e, 2 * dh:2 * dh + BIAS_LANES, :] = jnp.where(ones, 1.0, 0.0).astype(BF16)

    def scores(ki):
        r0 = pl.multiple_of(ki * tk, tk)
        lhs = jnp.concatenate([k_ref[pl.ds(r0, tk), :], kb_ref[pl.ds(r0, tk), :]], axis=1)
        return [_dot(lhs, rhs_ref[e]) for e in range(2)]

    def update(e, ki, s, m, l, acc):
        m_new = jnp.maximum(m, jnp.max(s, axis=0, keepdims=True))
        alpha = jnp.exp2(m - m_new)
        pr = jnp.exp2(s - m_new)
        l = alpha * l + jnp.sum(pr, axis=0, keepdims=True)
        acc = alpha * acc + _dot(vT_ref[e, ki], pr.astype(BF16))
        return m_new, l, acc

    def step(ki, carry):
        ss = scores(ki)
        out = []
        for e in range(2):
            out.extend(update(e, ki, ss[e], *carry[3 * e:3 * e + 3]))
        return tuple(out)

    init = []
    for e in range(2):
        init.extend([jnp.full((1, tq), -jnp.inf, F32), jnp.zeros((1, tq), F32),
                     jnp.zeros((dh, tq), F32)])
    carry = lax.fori_loop(0, qi, step, tuple(init))

    ss = scores(qi)
    kpos = lax.broadcasted_iota(jnp.int32, (tk, tq), 0)
    qpos = lax.broadcasted_iota(jnp.int32, (tk, tq), 1)
    outs = []
    for e in range(2):
        s = jnp.where(kpos <= qpos, ss[e], -jnp.inf)
        m, l, acc = update(e, qi, s, *carry[3 * e:3 * e + 3])
        outs.append(acc / l)
    o_ref[...] = jnp.concatenate(outs, axis=0).T


def _fox_prompt(qT, ktm, kb, vT):
    B, nb, aw, tq = qT.shape
    _, ah, _, dh, _ = vT.shape
    Lp = ktm.shape[1]
    npair = ah // 2
    in_specs = [
        pl.BlockSpec((None, None, 2 * dh, tq), lambda b, p, i: (b, i, p, 0)),
        pl.BlockSpec((None, Lp, 2 * dh), lambda b, p, i: (b, 0, p)),
        pl.BlockSpec((None, Lp, BIAS_LANES), lambda b, p, i: (b, 0, 0)),
        pl.BlockSpec((None, 2, nb, dh, tq), lambda b, p, i: (b, p, 0, 0, 0)),
    ]
    return pl.pallas_call(
        functools.partial(_fox_prompt_body, dh),
        grid=(B, npair, nb), in_specs=in_specs,
        out_specs=pl.BlockSpec((None, tq, 2 * dh), lambda b, p, i: (b, i, p)),
        out_shape=jax.ShapeDtypeStruct((B, Lp, aw), F32),
        scratch_shapes=[pltpu.VMEM((2, 2 * dh + BIAS_LANES, tq), BF16)],
        compiler_params=_cparams(3), name="fox_prompt",
    )(qT, ktm, kb, vT)


def _hgrn_tables():
    C = HG_CHUNK
    levels = []
    m = C // 2
    while m >= 1:
        levels.append(m)
        m //= 2
    LV = -np.ones((C, C), np.int32)
    for li, m in enumerate(levels):
        for t in range(C):
            for s in range(t):
                if t // (2 * m) == s // (2 * m) and t % (2 * m) >= m and s % (2 * m) < m:
                    LV[t, s] = li
    for t in range(C):
        LV[t, t] = len(levels)
    return LV, levels


def _level_exponent(cum, lf, m, rowi):
    C, W = cum.shape
    if m >= 4:
        g = cum.reshape(C // (2 * m), 2 * m, W)
        mid = jnp.broadcast_to(g[:, m - 1:m, :], g.shape).reshape(C, W)
        return jnp.where((rowi & m) != 0, cum - mid, mid - cum)
    if m == 2:
        nxt = pltpu.roll(lf, C - 1, 0)
        prv = pltpu.roll(lf, 1, 0)
        r4 = rowi & 3
        return jnp.where(r4 == 0, nxt, jnp.where(r4 == 1, 0.0, jnp.where(r4 == 2, lf, prv + lf)))
    return jnp.where((rowi & 1) != 0, lf, 0.0)


def _hgrn_prompt_body(L, nh, levels, q_ref, k_ref, v_ref, lf_ref, tri_ref, lv_ref,
                      o_ref, s_ref, st_ref, qh_ref, u_ref, dec_ref):
    i = pl.program_id(1)
    nblk = pl.num_programs(1)
    tm = q_ref.shape[0]
    C = HG_CHUNK
    hd = q_ref.shape[1] // nh
    nchunk = tm // C

    @pl.when(i == 0)
    def _():
        st_ref[...] = jnp.zeros_like(st_ref)

    tri = tri_ref[...]
    lv = lv_ref[...]
    rowi = lax.broadcasted_iota(jnp.int32, (C, hd), 0)

    def chunk(c, carry):
        r0 = pl.multiple_of(c * C, C)
        valid = (i * tm + r0 + rowi) < L
        for h in range(nh):
            sl = slice(h * hd, (h + 1) * hd)
            q = q_ref[pl.ds(r0, C), sl]
            k = k_ref[pl.ds(r0, C), sl]
            v = jnp.where(valid, v_ref[pl.ds(r0, C), sl], 0.0)
            lf = jnp.where(valid, lf_ref[pl.ds(r0, C), sl], 0.0)
            l0, l1, l2 = _split3(lf)
            cum = _dot(tri, l0) + _dot(tri, l1) + _dot(tri, l2)
            last = cum[C - 1:C, :]
            qh_ref[pl.ds(r0, C), sl] = (q * jnp.exp(cum)).astype(BF16)
            kh = (k * jnp.exp(last - cum)).astype(BF16)
            vb = v.astype(BF16)
            u_ref[c, h] = _dot_tn(vb, kh)
            dec_ref[c, h] = jnp.broadcast_to(jnp.exp(last), (8, hd))
            a = jnp.where(lv == len(levels), _dot_nt(q.astype(BF16), k.astype(BF16)), 0.0)
            for li, m in enumerate(levels):
                e = _level_exponent(cum, lf, m, rowi)
                w = (jnp.where((rowi & m) != 0, q, k) * jnp.exp(e)).astype(BF16)
                a = a + jnp.where(lv == li, _dot_nt(w, w), 0.0)
            o_ref[pl.ds(r0, C), sl] = _dot(a.astype(BF16), vb)
        return carry

    lax.fori_loop(0, nchunk, chunk, 0, unroll=2)

    for h in range(nh):
        sl = slice(h * hd, (h + 1) * hd)
        st = st_ref[h]
        for c in range(nchunk):
            rows = slice(c * C, (c + 1) * C)
            o_ref[rows, sl] += _dot_nt(qh_ref[rows, sl], st.astype(BF16))
            st = st * dec_ref[c, h][0:1, :] + u_ref[c, h]
        st_ref[h] = st

    @pl.when(i == nblk - 1)
    def _():
        for h in range(nh):
            s_ref[h] = st_ref[h].T


def _hgrn_prompt(hq, hk, hv, hlf, tri, lv, levels, L, nh):
    B, Lp, hw = hq.shape
    tm = ROW_TILE
    nb = Lp // tm
    hd = hw // nh
    nchunk = tm // HG_CHUNK
    row = pl.BlockSpec((None, tm, hw), lambda b, i: (b, i, 0))
    return pl.pallas_call(
        functools.partial(_hgrn_prompt_body, L, nh, levels),
        grid=(B, nb),
        in_specs=[row, row, row, row, _full(tri.shape), _full(lv.shape)],
        out_specs=(row, pl.BlockSpec((None, nh, hd, hd), lambda b, i: (b, 0, 0, 0))),
        out_shape=(jax.ShapeDtypeStruct((B, Lp, hw), F32),
                   jax.ShapeDtypeStruct((B, nh, hd, hd), F32)),
        scratch_shapes=[pltpu.VMEM((nh, hd, hd), F32), pltpu.VMEM((tm, hw), BF16),
                        pltpu.VMEM((nchunk, nh, hd, hd), F32), pltpu.VMEM((nchunk, nh, 8, hd), F32)],
        compiler_params=_cparams(2), name="hgrn_prompt",
    )(hq, hk, hv, hlf, tri, lv)


def _merge_body(nh, x_ref, att_ref, ho_ref, gate_ref, ga_ref, gh_ref, gf_ref, w_ref,
                xo_ref, h2_ref):
    att = att_ref[...]
    a = _rms_rows(att, ga_ref[...])
    ho = ho_ref[...]
    hd = ho.shape[1] // nh
    gh = gh_ref[...]
    parts = [a]
    for h in range(nh):
        sl = slice(h * hd, (h + 1) * hd)
        parts.append(_rms_rows(ho[:, sl], gh[:, sl]) * gate_ref[:, sl])
    m = jnp.concatenate(parts, axis=1).astype(BF16)
    xn = x_ref[...] + _dot(m, w_ref[...])
    xo_ref[...] = xn
    h2_ref[...] = _rms_rows(xn, gf_ref[...]).astype(BF16)


def _merge(x, att, ho, gate, ga, gh, gf, w, nh, tm):
    R, D = x.shape
    aw = att.shape[1]
    hw = ho.shape[1]
    row = lambda width: pl.BlockSpec((tm, width), lambda i: (i, 0))
    return pl.pallas_call(
        functools.partial(_merge_body, nh),
        grid=(R // tm,),
        in_specs=[row(D), row(aw), row(hw), row(hw), _full((1, aw)), _full((1, hw)),
                  _full((1, D)), _resident(w.shape)],
        out_specs=(row(D), row(D)),
        out_shape=(jax.ShapeDtypeStruct((R, D), F32), jax.ShapeDtypeStruct((R, D), BF16)),
        compiler_params=_cparams(1), name="merge_out",
    )(x, att, ho, gate, ga, gh, gf, w)


def _convffn_body(seq_len, state_rows, x_ref, h2_ref, wup_ref, cw_ref, cb_ref, wdn_ref, *rest):
    if seq_len is None:
        xo_ref, cs_ref, halo_ref, buf_ref, acc_ref = rest
    else:
        p1_ref, p2_ref, xo_ref, a_ref, halo_ref, buf_ref, acc_ref = rest
    i = pl.program_id(1)
    tm = h2_ref.shape[0]
    dff = wdn_ref.shape[0]
    tn = FF_TILE
    h2 = h2_ref[...]

    @pl.when(i == 0)
    def _():
        halo_ref[...] = jnp.zeros_like(halo_ref)

    if seq_len is not None:
        tmod = lax.broadcasted_iota(jnp.int32, (tm, tn), 0) % seq_len

    for c in range(dff // tn):
        cs = slice(c * tn, (c + 1) * tn)
        a = _dot(h2, wup_ref[:, c * tn:(c + 1) * tn])
        b = _dot(h2, wup_ref[:, dff + c * tn:dff + (c + 1) * tn])
        buf_ref[0:8, :] = halo_ref[:, cs]
        buf_ref[8:tm + 8, :] = a
        halo_ref[:, cs] = a[tm - 8:tm, :]
        a1 = buf_ref[7:tm + 7, :]
        a2 = buf_ref[6:tm + 6, :]
        if seq_len is None:
            lo, blk = state_rows
            @pl.when(i == blk)
            def _():
                cs_ref[:, cs] = a[lo:lo + 2, :]
        else:
            a1 = jnp.where(tmod >= 1, a1, p1_ref[:, cs])
            a2 = jnp.where(tmod >= 2, a2, p2_ref[:, cs])
            a_ref[:, cs] = a
        cv = cb_ref[:, cs] + cw_ref[0:1, cs] * a2 + cw_ref[1:2, cs] * a1 + cw_ref[2:3, cs] * a
        gt = (cv * _sigmoid(cv) * b).astype(BF16)
        y = _dot(gt, wdn_ref[c * tn:(c + 1) * tn, :])
        if c == 0:
            acc_ref[...] = y
        else:
            acc_ref[...] += y
    xo_ref[...] = x_ref[...] + acc_ref[...]


def _convffn_prompt(x, h2, wup, cw, cb, wdn, L):
    B, Lp, D = x.shape
    tm = ROW_TILE
    nb = Lp // tm
    dff = wdn.shape[0]
    row = pl.BlockSpec((None, tm, D), lambda b, i: (b, i, 0))
    state_rows = ((L - 2) % tm, (L - 2) // tm)
    return pl.pallas_call(
        functools.partial(_convffn_body, None, state_rows),
        grid=(B, nb),
        in_specs=[row, row, _resident(wup.shape), _full(cw.shape), _full(cb.shape),
                  _resident(wdn.shape)],
        out_specs=(row, pl.BlockSpec((None, 2, dff), lambda b, i: (b, 0, 0))),
        out_shape=(jax.ShapeDtypeStruct((B, Lp, D), F32), jax.ShapeDtypeStruct((B, 2, dff), F32)),
        scratch_shapes=[pltpu.VMEM((8, dff), F32), pltpu.VMEM((tm + 8, FF_TILE), F32),
                        pltpu.VMEM((tm, D), F32)],
        compiler_params=_cparams(2), name="convffn_prompt",
    )(x, h2, wup, cw, cb, wdn)


def _convffn_sample(x, h2, wup, cw, cb, wdn, p1, p2, seq_len):
    R, D = x.shape
    dff = wdn.shape[0]
    full2 = lambda shape: pl.BlockSpec(shape, lambda b, i: (0, 0))
    return pl.pallas_call(
        functools.partial(_convffn_body, seq_len, None),
        grid=(1, 1),
        in_specs=[full2((R, D)), full2((R, D)), full2(wup.shape), full2(cw.shape),
                  full2(cb.shape), full2(wdn.shape), full2((R, dff)), full2((R, dff))],
        out_specs=(full2((R, D)), full2((R, dff))),
        out_shape=(jax.ShapeDtypeStruct((R, D), F32), jax.ShapeDtypeStruct((R, dff), F32)),
        scratch_shapes=[pltpu.VMEM((8, dff), F32), pltpu.VMEM((R + 8, FF_TILE), F32),
                        pltpu.VMEM((R, D), F32)],
        compiler_params=_cparams(2), name="convffn_sample",
    )(x, h2, wup, cw, cb, wdn, p1, p2)


def _sample_tables(page, ah, T, npp):
    idx = np.arange(page * ah)
    tok, head = idx // ah, idx % ah
    same = head[:, None] == head[None, :]
    suffix = (same & (tok[:, None] > tok[None, :])).astype(np.float32)
    expand = (np.arange(NEW_LANES)[:, None] == head[None, :]).astype(np.float32)
    new = np.zeros((NEW_LANES, NEW_LANES), np.float32)
    n = T * ah
    new[:n, :n] = (same[:n, :n] & (tok[:n, None] <= tok[None, :n])).astype(np.float32)
    return suffix, expand, new


def _fox_sample_body(npp, ah, dh, T, pt_ref, qm_ref, kn_ref, vn_ref, alf_ref, new_ref, suf_ref,
                     exp_ref, *rest):
    k_refs = rest[0:npp]
    v_refs = rest[npp:2 * npp]
    lf_refs = rest[2 * npp:3 * npp]
    o_ref, m_ref, l_ref, acc_ref, tail_ref = rest[3 * npp:]
    g = pl.program_id(1)
    ng = pl.num_programs(1)
    R = T * ah
    PK = k_refs[0].shape[0]

    qm = (qm_ref[...] * (float(dh) ** -0.5)).astype(BF16)

    @pl.when(g == 0)
    def _():
        s = _dot_nt(qm, kn_ref[...].astype(BF16))
        a0, a1, a2 = _split3(alf_ref[...])
        nw = new_ref[...]
        cn = _dot(a0, nw) + _dot(a1, nw) + _dot(a2, nw)
        lane = lax.broadcasted_iota(jnp.int32, s.shape, 1)
        row = lax.broadcasted_iota(jnp.int32, s.shape, 0)
        ok = (lane % ah == row // T) & (lane // ah <= row % T) & (lane < T * ah)
        s = jnp.where(ok, s - cn[0:1, :], -jnp.inf)
        m = jnp.max(s, axis=1, keepdims=True)
        pr = jnp.exp(s - m)
        m_ref[...] = m
        l_ref[...] = jnp.sum(pr, axis=1, keepdims=True)
        acc_ref[...] = _dot(pr.astype(BF16), vn_ref[...].astype(BF16))
        tail_ref[...] = jnp.zeros_like(tail_ref)

    lf = jnp.concatenate([r[...] for r in lf_refs], axis=0)
    b0, b1, b2 = _split3(lf)
    suf = suf_ref[...]
    within = _dot(b0, suf) + _dot(b1, suf) + _dot(b2, suf)
    lane = lax.broadcasted_iota(jnp.int32, (npp, NEW_LANES), 1)
    tot = jnp.where(lane < ah, (within + lf)[:, 0:NEW_LANES], 0.0)
    run = tail_ref[...]
    rows = [None] * npp
    for j in reversed(range(npp)):
        rows[j] = run
        run = run + tot[j:j + 1, :]
    tail_ref[...] = run
    c0, c1, c2 = _split3(jnp.concatenate(rows, axis=0))
    ex = exp_ref[...]
    bias = within + _dot(c0, ex) + _dot(c1, ex) + _dot(c2, ex)

    row_h = lax.broadcasted_iota(jnp.int32, (R, PK), 0) // T
    lane_h = lax.broadcasted_iota(jnp.int32, (R, PK), 1) % ah
    valid = row_h == lane_h
    s_parts = []
    for j in range(npp):
        kj = k_refs[j][...].astype(BF16)
        s_parts.append(jnp.where(valid, _dot_nt(qm, kj) + bias[j:j + 1, :], -jnp.inf))
    s = jnp.concatenate(s_parts, axis=1)
    m_old = m_ref[...]
    m_new = jnp.maximum(m_old, jnp.max(s, axis=1, keepdims=True))
    alpha = jnp.exp(m_old - m_new)
    pr = jnp.exp(s - m_new)
    l_ref[...] = alpha * l_ref[...] + jnp.sum(pr, axis=1, keepdims=True)
    m_ref[...] = m_new
    prb = pr.astype(BF16)
    acc = alpha * acc_ref[...]
    for j in range(npp):
        acc = acc + _dot(prb[:, j * PK:(j + 1) * PK], v_refs[j][...].astype(BF16))
    acc_ref[...] = acc

    @pl.when(g == ng - 1)
    def _():
        o_ref[...] = acc / l_ref[...]


def _fox_sample(layer, page_table, qm, kn, vn, alf, cache_k, cache_v, cache_lf, new, suf, ex, ah, T):
    Bd, R, dh = qm.shape
    n_pages = page_table.shape[1]
    PK = cache_k.shape[2]
    npp = min(PAGES_PER_STEP, n_pages)
    ng = n_pages // npp

    def page_map(r):
        return lambda b, g, pt: (layer, pt[b, n_pages - (g + 1) * npp + r], 0, 0)

    seq = lambda shape: pl.BlockSpec((None,) + shape, lambda b, g, pt: (b, 0, 0))
    const = lambda shape: pl.BlockSpec(shape, lambda b, g, pt: (0,) * len(shape),
                                       pipeline_mode=pl.Buffered(1))
    in_specs = [seq((R, dh)), seq((NEW_LANES, dh)), seq((NEW_LANES, dh)), seq((8, NEW_LANES)),
                const(new.shape), const(suf.shape), const(ex.shape)]
    in_specs += [pl.BlockSpec((None, None, PK, dh), page_map(r)) for r in range(npp)]
    in_specs += [pl.BlockSpec((None, None, PK, dh), page_map(r)) for r in range(npp)]
    in_specs += [pl.BlockSpec((None, None, 1, PK), page_map(r)) for r in range(npp)]
    grid_spec = pltpu.PrefetchScalarGridSpec(
        num_scalar_prefetch=1, grid=(Bd, ng), in_specs=in_specs,
        out_specs=pl.BlockSpec((None, R, dh), lambda b, g, pt: (b, 0, 0)),
        scratch_shapes=[pltpu.VMEM((R, 1), F32), pltpu.VMEM((R, 1), F32),
                        pltpu.VMEM((R, dh), F32), pltpu.VMEM((1, NEW_LANES), F32)])
    return pl.pallas_call(
        functools.partial(_fox_sample_body, npp, ah, dh, T),
        grid_spec=grid_spec, out_shape=jax.ShapeDtypeStruct((Bd, R, dh), F32),
        compiler_params=_cparams(2), name="fox_sample",
    )(page_table, qm, kn, vn, alf, new, suf, ex, *([cache_k] * npp), *([cache_v] * npp),
      *([cache_lf] * npp))


def _hgrn_sample_body(nh, T, q_ref, k_ref, v_ref, lf_ref, s0_ref, o_ref, s1_ref):
    hd = s0_ref.shape[-1]
    outs = [[] for _ in range(T)]
    for h in range(nh):
        sl = slice(h * hd, (h + 1) * hd)
        q = [q_ref[t:t + 1, sl] for t in range(T)]
        k = [k_ref[t:t + 1, sl] for t in range(T)]
        v = [v_ref[t:t + 1, sl] for t in range(T)]
        cum = [lf_ref[0:1, sl]]
        for t in range(1, T):
            cum.append(cum[-1] + lf_ref[t:t + 1, sl])
        s0 = s0_ref[h]
        pad = [jnp.zeros((1, hd), F32)] * (8 - T)
        qh = jnp.concatenate([q[t] * jnp.exp(cum[t]) for t in range(T)] + pad, axis=0)
        o_inter = _dot(qh.astype(BF16), s0.astype(BF16))
        for t in range(T):
            o = o_inter[t:t + 1, :]
            for s in range(t + 1):
                a = jnp.sum(q[t] * k[s] * jnp.exp(cum[t] - cum[s]), axis=1, keepdims=True)
                o = o + a * v[s]
            outs[t].append(o)
        kh = jnp.concatenate([k[s] * jnp.exp(cum[T - 1] - cum[s]) for s in range(T)] + pad, axis=0)
        vv = jnp.concatenate(v + pad, axis=0)
        decay_col = jnp.broadcast_to(jnp.exp(cum[T - 1]), (hd, hd)).T
        s1_ref[h] = decay_col * s0 + _dot_tn(kh.astype(BF16), vv.astype(BF16))
    for t in range(T):
        o_ref[t:t + 1, :] = jnp.concatenate(outs[t], axis=1)


def _hgrn_sample(hq, hk, hv, hlf, s0, nh):
    Bd, T, hw = hq.shape
    hd = hw // nh
    seq = pl.BlockSpec((None, T, hw), lambda b: (b, 0, 0))
    st = pl.BlockSpec((None, nh, hd, hd), lambda b: (b, 0, 0, 0))
    return pl.pallas_call(
        functools.partial(_hgrn_sample_body, nh, T),
        grid=(Bd,), in_specs=[seq, seq, seq, seq, st], out_specs=(seq, st),
        out_shape=(jax.ShapeDtypeStruct((Bd, T, hw), F32),
                   jax.ShapeDtypeStruct((Bd, nh, hd, hd), F32)),
        compiler_params=_cparams(1), name="hgrn_sample",
    )(hq, hk, hv, hlf, s0)


def _final_norm_body(x_ref, g_ref, o_ref):
    o_ref[...] = _rms_rows(x_ref[...], g_ref[...])


def _final_norm_shifted_body(x_ref, g_ref, o_ref):
    o_ref[...] = _rms_rows(x_ref[0], g_ref[...])


def _final_norm_prompt(x, g, seq, tm):
    B, _, D = x.shape
    return pl.pallas_call(
        _final_norm_shifted_body, grid=(B, seq // tm),
        in_specs=[pl.BlockSpec((pl.Element(1), pl.Element(tm), pl.Element(D)),
                               lambda b, i: (b, pl.multiple_of(N_META + i * tm, 8), 0)),
                  _full((1, D))],
        out_specs=pl.BlockSpec((None, tm, D), lambda b, i: (b, i, 0)),
        out_shape=jax.ShapeDtypeStruct((B, seq, D), F32),
        compiler_params=_cparams(2), name="final_norm_prompt",
    )(x, g)


def _final_norm_sample(x, g):
    return pl.pallas_call(
        _final_norm_body, grid=(1,), in_specs=[_full(x.shape), _full(g.shape)],
        out_specs=_full(x.shape), out_shape=jax.ShapeDtypeStruct(x.shape, F32),
        compiler_params=_cparams(1), name="final_norm_sample",
    )(x, g)


def _expand_heads(vec, ah):
    rep = jnp.repeat(vec.astype(F32), BIAS_STRIDE)
    return jnp.pad(rep, (0, BIAS_LANES - ah * BIAS_STRIDE))[None, :]


def kernel(x_prompt, x_sample, cache_k, cache_v, cache_logf, state_hgrn, state_conv, page_table,
           meta_tokens, g_mix, w_in, b_f, hg_lb, g_att_out, g_hg_out, w_out, g_ffn, w_up,
           conv_w, conv_b, w_down, g_final):
    B, seq, D = x_prompt.shape
    Bd, T, _ = x_sample.shape
    depth = w_in.shape[0]
    ah = b_f.shape[1]
    aw = cache_k.shape[3] * cache_k.shape[4]
    hw = hg_lb.shape[1]
    nh = state_hgrn.shape[2]
    dff = w_down.shape[1]
    n_pool, page = cache_k.shape[1], cache_k.shape[2]
    L = N_META + seq
    tm = ROW_TILE
    Lp = -(-L // tm) * tm
    assert ah * BIAS_STRIDE <= BIAS_LANES and ah % 2 == 0 and dff % FF_TILE == 0
    assert page_table.shape[1] % min(PAGES_PER_STEP, page_table.shape[1]) == 0 and T <= 8

    sm = jax.nn.softmax(hg_lb.astype(F32), axis=0)
    lb_all = jnp.cumsum(sm, axis=0) - sm[0:1]

    tri = jnp.asarray(np.tril(np.ones((tm, tm), np.float32)), BF16)
    tri_c = jnp.asarray(np.tril(np.ones((HG_CHUNK, HG_CHUNK), np.float32)), BF16)
    lv_np, levels = _hgrn_tables()
    lv = jnp.asarray(lv_np, jnp.int32)
    npp = min(PAGES_PER_STEP, page_table.shape[1])
    suf_np, ex_np, new_np = _sample_tables(page, ah, T, npp)
    suf, ex, new = (jnp.asarray(a, BF16) for a in (suf_np, ex_np, new_np))

    dh = aw // ah
    cache_k4 = cache_k.reshape(depth, n_pool, page * ah, dh)
    cache_v4 = cache_v.reshape(depth, n_pool, page * ah, dh)
    cache_lf = cache_logf.reshape(depth, n_pool, 1, page * ah)

    xp = jnp.concatenate([jnp.broadcast_to(meta_tokens[None].astype(F32), (B, N_META, D)),
                          x_prompt, jnp.zeros((B, Lp - L, D), F32)], axis=1)
    xs = x_sample.reshape(Bd * T, D)

    outs = [[] for _ in range(10)]
    for l in range(depth):
        wi = w_in[l]
        o = 3 * aw + ah
        zf_cols = wi[:, 3 * aw:o]
        w_arr = jnp.concatenate(
            [wi[:, :3 * aw], wi[:, o:o + 4 * hw],
             jnp.pad(jnp.repeat(zf_cols, BIAS_STRIDE, axis=1),
                     ((0, 0), (0, BIAS_LANES - ah * BIAS_STRIDE))),
             jnp.pad(zf_cols, ((0, 0), (0, BIAS_LANES - ah)))], axis=1).astype(BF16)
        bfe = _expand_heads(b_f[l], ah)
        bfp = jnp.pad(b_f[l].astype(F32), (0, BIAS_LANES - ah))[None, :]
        lb = lb_all[l][None, :]
        g1 = g_mix[l][None, :].astype(F32)
        ga = g_att_out[l][None, :].astype(F32)
        gh = g_hg_out[l][None, :].astype(F32)
        gf = g_ffn[l][None, :].astype(F32)
        wo = w_out[l].astype(BF16)
        wu = w_up[l].astype(BF16)
        wd = w_down[l].astype(BF16)
        cw = conv_w[l].astype(F32)
        cb = conv_b[l][None, :].astype(F32)

        (qT, ktm, kb, vT, k_new, v_new, alf, hq, hk, hv, hlf, gate) = _in_proj_prompt(
            xp, g1, w_arr, bfe, bfp, lb, tri, L, aw, hw, ah, nh)
        att = _fox_prompt(qT, ktm, kb, vT)
        ho, s_p = _hgrn_prompt(hq, hk, hv, hlf, tri_c, lv, levels, L, nh)
        xp2, h2 = _merge(xp.reshape(B * Lp, D), att.reshape(B * Lp, aw), ho.reshape(B * Lp, hw),
                         gate.reshape(B * Lp, hw), ga, gh, gf, wo, nh, tm)
        xp, c_p = _convffn_prompt(xp2.reshape(B, Lp, D), h2.reshape(B, Lp, D), wu, cw, cb, wd, L)
        outs[0].append(k_new.reshape(B, L, ah, aw // ah))
        outs[1].append(v_new.reshape(B, L, ah, aw // ah))
        outs[2].append(alf)
        outs[3].append(s_p)
        outs[4].append(c_p)

        (sq, sk, sv, salf, shq, shk, shv, shlf, sgate) = _in_proj_sample(
            xs, g1, w_arr, bfe, bfp, lb, aw, hw, ah, nh)
        qm = sq.reshape(Bd, T, ah, dh).transpose(0, 2, 1, 3).reshape(Bd, ah * T, dh)
        padn = ((0, 0), (0, NEW_LANES - T * ah), (0, 0))
        kn = jnp.pad(sk.reshape(Bd, T * ah, dh), padn)
        vn = jnp.pad(sv.reshape(Bd, T * ah, dh), padn)
        alf_n = jnp.pad(salf.reshape(Bd, 1, T * ah), ((0, 0), (0, 7), (0, NEW_LANES - T * ah)))
        att_s = _fox_sample(l, page_table, qm, kn, vn, alf_n, cache_k4, cache_v4, cache_lf,
                            new, suf, ex, ah, T)
        att_s = att_s.reshape(Bd, ah, T, dh).transpose(0, 2, 1, 3)
        ho_s, s_s = _hgrn_sample(shq.reshape(Bd, T, hw), shk.reshape(Bd, T, hw),
                                 shv.reshape(Bd, T, hw), shlf.reshape(Bd, T, hw),
                                 state_hgrn[l].astype(F32), nh)
        xs2, h2s = _merge(xs, att_s.reshape(Bd * T, aw), ho_s.reshape(Bd * T, hw), sgate,
                          ga, gh, gf, wo, nh, Bd * T)
        prev = state_conv[l].astype(F32)
        zeros = jnp.zeros((Bd, T - 1, dff), F32)
        p1 = jnp.concatenate([prev[:, 1:2], zeros], axis=1).reshape(Bd * T, dff)
        p2 = jnp.concatenate([prev, zeros[:, :T - 2]], axis=1).reshape(Bd * T, dff)
        xs, a_s = _convffn_sample(xs2, h2s, wu, cw, cb, wd, p1, p2, T)
        outs[5].append(sk.reshape(Bd, T, ah, aw // ah))
        outs[6].append(sv.reshape(Bd, T, ah, aw // ah))
        outs[7].append(salf.reshape(Bd, T, ah))
        outs[8].append(s_s)
        outs[9].append(a_s.reshape(Bd, T, dff)[:, T - 2:])

    gfin = g_final[None, :].astype(F32)
    y_prompt = _final_norm_prompt(xp, gfin, seq, 512 if seq % 512 == 0 else seq)
    y_sample = _final_norm_sample(xs, gfin).reshape(Bd, T, D)
    return (y_prompt, y_sample) + tuple(jnp.stack(o) for o in outs)
```

```python
import functools

import numpy as np
import jax
import jax.numpy as jnp
from jax import lax
from jax.experimental import pallas as pl
from jax.experimental.pallas import tpu as pltpu

F32 = jnp.float32
BF16 = jnp.bfloat16

N_META = 16
EPS = 1e-6
LOG2E = 1.4426950408889634
ROW_TILE = 640
HG_CHUNK = 64
FF_TILE = 256
BIAS_LANES = 128
BIAS_STRIDE = 16
PAGES_PER_STEP = 16
NEW_LANES = 128
VMEM_LIMIT = 56 * 1024 * 1024


def _cparams(n_axes):
    return pltpu.CompilerParams(dimension_semantics=("arbitrary",) * n_axes,
                                vmem_limit_bytes=VMEM_LIMIT)


def _full(shape):
    nd = len(shape)
    return pl.BlockSpec(shape, lambda *_: (0,) * nd)


def _resident(shape):
    nd = len(shape)
    return pl.BlockSpec(shape, lambda *_: (0,) * nd, pipeline_mode=pl.Buffered(1))


def _split3(x):
    hi = x.astype(BF16)
    r1 = x - hi.astype(F32)
    mid = r1.astype(BF16)
    lo = (r1 - mid.astype(F32)).astype(BF16)
    return hi, mid, lo


def _log_sigmoid(x):
    return jnp.minimum(x, 0.0) - jnp.log1p(jnp.exp(-jnp.abs(x)))


def _sigmoid(x):
    return 1.0 / (1.0 + jnp.exp(-x))


def _rms_rows(x, g):
    return x * lax.rsqrt(jnp.mean(x * x, axis=-1, keepdims=True) + EPS) * g


def _dot(a, b):
    return jnp.dot(a, b, preferred_element_type=F32)


def _dot_nt(a, b):
    return lax.dot_general(a, b, (((1,), (1,)), ((), ())), preferred_element_type=F32)


def _dot_tn(a, b):
    return lax.dot_general(a, b, (((0,), (0,)), ((), ())), preferred_element_type=F32)


def _in_proj_body(prompt, aw, hw, ah, dh, hd, x_ref, g_ref, w_ref, bfe_ref, bfp_ref, lb_ref, *rest):
    if prompt:
        (tri_ref, qT_ref, ktm_ref, kb_ref, vT_ref, ko_ref, vo_ref, alf_ref,
         hq_ref, hk_ref, hv_ref, hlf_ref, gate_ref, carry_ref) = rest
    else:
        (q_ref, ko_ref, vo_ref, alf_ref,
         hq_ref, hk_ref, hv_ref, hlf_ref, gate_ref) = rest

    hb = _rms_rows(x_ref[...], g_ref[...]).astype(BF16)

    def col(c0, n):
        return _dot(hb, w_ref[:, c0:c0 + n])

    q = col(0, aw)
    k = col(aw, aw)
    v = col(2 * aw, aw)
    o = 3 * aw
    hq = col(o, hw)
    hf = col(o + hw, hw)
    hi = col(o + 2 * hw, hw)
    hg = col(o + 3 * hw, hw)
    zfp = col(o + 4 * hw + BIAS_LANES, BIAS_LANES)

    ko_ref[...] = k
    vo_ref[...] = v
    alfp = _log_sigmoid(zfp + bfp_ref[...])
    alf_ref[...] = alfp[:, :ah]

    lb = lb_ref[...]
    f = lb + (1.0 - lb) * _sigmoid(hf)
    hq_ref[...] = hq * (float(hd) ** -0.5)
    hk_ref[...] = 1.0 - f
    hv_ref[...] = hi
    hlf_ref[...] = jnp.log(f)
    gate_ref[...] = hg * _sigmoid(hg)

    if not prompt:
        q_ref[...] = q
        return

    alfe = _log_sigmoid(col(o + 4 * hw, BIAS_LANES) + bfe_ref[...])
    i = pl.program_id(1)

    @pl.when(i == 0)
    def _():
        carry_ref[...] = jnp.zeros_like(carry_ref)

    tri = tri_ref[...]
    p0, p1, p2 = _split3(alfe)
    cum = _dot(tri, p0) + _dot(tri, p1) + _dot(tri, p2) + carry_ref[...]
    tm = cum.shape[0]
    carry_ref[...] = cum[tm - 1:tm, :]
    n0, n1, n2 = _split3(cum * (-LOG2E))
    lane = lax.broadcasted_iota(jnp.int32, cum.shape, 1) % BIAS_STRIDE
    zero = jnp.zeros_like(n0)
    kb_ref[...] = jnp.where(lane == 0, n0, jnp.where(lane == 1, n1, jnp.where(lane == 2, n2, zero)))

    qT_ref[...] = (q * (float(dh) ** -0.5 * LOG2E)).T.astype(BF16)
    ktm_ref[...] = k.astype(BF16)
    vt = v.T
    for h in range(ah):
        vT_ref[h] = vt[h * dh:(h + 1) * dh, :].astype(BF16)


def _in_proj_prompt(x, g, w, bfe, bfp, lb, tri, L, aw, hw, ah, nh):
    B, Lp, D = x.shape
    tm = ROW_TILE
    nb = Lp // tm
    dh = aw // ah
    nc = w.shape[1]
    row = lambda width: pl.BlockSpec((None, tm, width), lambda b, i: (b, i, 0))
    out_shape = (
        jax.ShapeDtypeStruct((B, nb, aw, tm), BF16),
        jax.ShapeDtypeStruct((B, Lp, aw), BF16),
        jax.ShapeDtypeStruct((B, Lp, BIAS_LANES), BF16),
        jax.ShapeDtypeStruct((B, ah, nb, dh, tm), BF16),
        jax.ShapeDtypeStruct((B, L, aw), F32),
        jax.ShapeDtypeStruct((B, L, aw), F32),
        jax.ShapeDtypeStruct((B, L, ah), F32),
        jax.ShapeDtypeStruct((B, Lp, hw), F32),
        jax.ShapeDtypeStruct((B, Lp, hw), F32),
        jax.ShapeDtypeStruct((B, Lp, hw), F32),
        jax.ShapeDtypeStruct((B, Lp, hw), F32),
        jax.ShapeDtypeStruct((B, Lp, hw), F32),
    )
    out_specs = (
        pl.BlockSpec((None, None, aw, tm), lambda b, i: (b, i, 0, 0)),
        row(aw), row(BIAS_LANES),
        pl.BlockSpec((None, ah, None, dh, tm), lambda b, i: (b, 0, i, 0, 0)),
        row(aw), row(aw), row(ah),
        row(hw), row(hw), row(hw), row(hw), row(hw),
    )
    in_specs = [row(D), _full((1, D)), _resident((D, nc)), _full((1, BIAS_LANES)),
                _full((1, BIAS_LANES)), _full((1, hw)), _resident((tm, tm))]
    return pl.pallas_call(
        functools.partial(_in_proj_body, True, aw, hw, ah, dh, hw // nh),
        grid=(B, nb), in_specs=in_specs, out_specs=out_specs, out_shape=out_shape,
        scratch_shapes=[pltpu.VMEM((1, BIAS_LANES), F32)],
        compiler_params=_cparams(2), name="in_proj_prompt",
    )(x, g, w, bfe, bfp, lb, tri)


def _in_proj_sample(x, g, w, bfe, bfp, lb, aw, hw, ah, nh):
    R, D = x.shape
    dh = aw // ah
    nc = w.shape[1]
    out_shape = (
        jax.ShapeDtypeStruct((R, aw), F32),
        jax.ShapeDtypeStruct((R, aw), F32),
        jax.ShapeDtypeStruct((R, aw), F32),
        jax.ShapeDtypeStruct((R, ah), F32),
        jax.ShapeDtypeStruct((R, hw), F32), jax.ShapeDtypeStruct((R, hw), F32),
        jax.ShapeDtypeStruct((R, hw), F32), jax.ShapeDtypeStruct((R, hw), F32),
        jax.ShapeDtypeStruct((R, hw), F32),
    )
    in_specs = [_full((R, D)), _full((1, D)), _full((D, nc)), _full((1, BIAS_LANES)),
                _full((1, BIAS_LANES)), _full((1, hw))]
    out_specs = tuple(_full(s.shape) for s in out_shape)
    return pl.pallas_call(
        functools.partial(_in_proj_body, False, aw, hw, ah, dh, hw // nh),
        grid=(1,), in_specs=in_specs, out_specs=out_specs, out_shape=out_shape,
        compiler_params=_cparams(1), name="in_proj_sample",
    )(x, g, w, bfe, bfp, lb)


def _fox_prompt_body(dh, qT_ref, k_ref, kb_ref, vT_ref, o_ref, rhs_ref):
    p = pl.program_id(1)
    qi = pl.program_id(2)
    tq = qT_ref.shape[1]
    tk = tq

    qT = qT_ref[...]
    row = lax.broadcasted_iota(jnp.int32, (2 * dh, tq), 0)
    brow = lax.broadcasted_iota(jnp.int32, (BIAS_LANES, tq), 0)
    for e in range(2):
        h = 2 * p + e
        rhs_ref[e, 0:2 * dh, :] = jnp.where(row // dh == e, qT, jnp.zeros_like(qT))
        ones = (brow // BIAS_STRIDE == h) & (brow % BIAS_STRIDE < 3)
        rhs_ref[e, 2 * dh:2 * dh + BIAS_LANES, :] = jnp.where(ones, 1.0, 0.0).astype(BF16)

    def scores(ki):
        r0 = pl.multiple_of(ki * tk, tk)
        lhs = jnp.concatenate([k_ref[pl.ds(r0, tk), :], kb_ref[pl.ds(r0, tk), :]], axis=1)
        return [_dot(lhs, rhs_ref[e]) for e in range(2)]

    def update(e, ki, s, m, l, acc):
        m_new = jnp.maximum(m, jnp.max(s, axis=0, keepdims=True))
        alpha = jnp.exp2(m - m_new)
        pr = jnp.exp2(s - m_new)
        l = alpha * l + jnp.sum(pr, axis=0, keepdims=True)
        acc = alpha * acc + _dot(vT_ref[e, ki], pr.astype(BF16))
        return m_new, l, acc

    def step(ki, carry):
        ss = scores(ki)
        out = []
        for e in range(2):
            out.extend(update(e, ki, ss[e], *carry[3 * e:3 * e + 3]))
        return tuple(out)

    init = []
    for e in range(2):
        init.extend([jnp.full((1, tq), -jnp.inf, F32), jnp.zeros((1, tq), F32),
                     jnp.zeros((dh, tq), F32)])
    carry = lax.fori_loop(0, qi, step, tuple(init))

    ss = scores(qi)
    kpos = lax.broadcasted_iota(jnp.int32, (tk, tq), 0)
    qpos = lax.broadcasted_iota(jnp.int32, (tk, tq), 1)
    outs = []
    for e in range(2):
        s = jnp.where(kpos <= qpos, ss[e], -jnp.inf)
        m, l, acc = update(e, qi, s, *carry[3 * e:3 * e + 3])
        outs.append(acc / l)
    o_ref[...] = jnp.concatenate(outs, axis=0).T


def _fox_prompt(qT, ktm, kb, vT):
    B, nb, aw, tq = qT.shape
    _, ah, _, dh, _ = vT.shape
    Lp = ktm.shape[1]
    npair = ah // 2
    in_specs = [
        pl.BlockSpec((None, None, 2 * dh, tq), lambda b, p, i: (b, i, p, 0)),
        pl.BlockSpec((None, Lp, 2 * dh), lambda b, p, i: (b, 0, p)),
        pl.BlockSpec((None, Lp, BIAS_LANES), lambda b, p, i: (b, 0, 0)),
        pl.BlockSpec((None, 2, nb, dh, tq), lambda b, p, i: (b, p, 0, 0, 0)),
    ]
    return pl.pallas_call(
        functools.partial(_fox_prompt_body, dh),
        grid=(B, npair, nb), in_specs=in_specs,
        out_specs=pl.BlockSpec((None, tq, 2 * dh), lambda b, p, i: (b, i, p)),
        out_shape=jax.ShapeDtypeStruct((B, Lp, aw), F32),
        scratch_shapes=[pltpu.VMEM((2, 2 * dh + BIAS_LANES, tq), BF16)],
        compiler_params=_cparams(3), name="fox_prompt",
    )(qT, ktm, kb, vT)


def _hgrn_tables():
    C = HG_CHUNK
    levels = []
    m = C // 2
    while m >= 1:
        levels.append(m)
        m //= 2
    LV = -np.ones((C, C), np.int32)
    for li, m in enumerate(levels):
        for t in range(C):
            for s in range(t):
                if t // (2 * m) == s // (2 * m) and t % (2 * m) >= m and s % (2 * m) < m:
                    LV[t, s] = li
    for t in range(C):
        LV[t, t] = len(levels)
    return LV, levels


def _level_exponent(cum, lf, m, rowi):
    C, W = cum.shape
    if m >= 4:
        g = cum.reshape(C // (2 * m), 2 * m, W)
        mid = jnp.broadcast_to(g[:, m - 1:m, :], g.shape).reshape(C, W)
        return jnp.where((rowi & m) != 0, cum - mid, mid - cum)
    if m == 2:
        nxt = pltpu.roll(lf, C - 1, 0)
        prv = pltpu.roll(lf, 1, 0)
        r4 = rowi & 3
        return jnp.where(r4 == 0, nxt, jnp.where(r4 == 1, 0.0, jnp.where(r4 == 2, lf, prv + lf)))
    return jnp.where((rowi & 1) != 0, lf, 0.0)


def _hgrn_prompt_body(L, nh, levels, q_ref, k_ref, v_ref, lf_ref, tri_ref, lv_ref,
                      o_ref, s_ref, st_ref, qh_ref, u_ref, dec_ref):
    i = pl.program_id(1)
    nblk = pl.num_programs(1)
    tm = q_ref.shape[0]
    C = HG_CHUNK
    hd = q_ref.shape[1] // nh
    nchunk = tm // C

    @pl.when(i == 0)
    def _():
        st_ref[...] = jnp.zeros_like(st_ref)

    tri = tri_ref[...]
    lv = lv_ref[...]
    rowi = lax.broadcasted_iota(jnp.int32, (C, hd), 0)

    def chunk(c, carry):
        r0 = pl.multiple_of(c * C, C)
        valid = (i * tm + r0 + rowi) < L
        for h in range(nh):
            sl = slice(h * hd, (h + 1) * hd)
            q = q_ref[pl.ds(r0, C), sl]
            k = k_ref[pl.ds(r0, C), sl]
            v = jnp.where(valid, v_ref[pl.ds(r0, C), sl], 0.0)
            lf = jnp.where(valid, lf_ref[pl.ds(r0, C), sl], 0.0)
            l0, l1, l2 = _split3(lf)
            cum = _dot(tri, l0) + _dot(tri, l1) + _dot(tri, l2)
            last = cum[C - 1:C, :]
            qh_ref[pl.ds(r0, C), sl] = (q * jnp.exp(cum)).astype(BF16)
            kh = (k * jnp.exp(last - cum)).astype(BF16)
            vb = v.astype(BF16)
            u_ref[c, h] = _dot_tn(vb, kh)
            dec_ref[c, h] = jnp.broadcast_to(jnp.exp(last), (8, hd))
            a = jnp.where(lv == len(levels), _dot_nt(q.astype(BF16), k.astype(BF16)), 0.0)
            for li, m in enumerate(levels):
                e = _level_exponent(cum, lf, m, rowi)
                w = (jnp.where((rowi & m) != 0, q, k) * jnp.exp(e)).astype(BF16)
                a = a + jnp.where(lv == li, _dot_nt(w, w), 0.0)
            o_ref[pl.ds(r0, C), sl] = _dot(a.astype(BF16), vb)
        return carry

    lax.fori_loop(0, nchunk, chunk, 0, unroll=2)

    for h in range(nh):
        sl = slice(h * hd, (h + 1) * hd)
        st = st_ref[h]
        for c in range(nchunk):
            rows = slice(c * C, (c + 1) * C)
            o_ref[rows, sl] += _dot_nt(qh_ref[rows, sl], st.astype(BF16))
            st = st * dec_ref[c, h][0:1, :] + u_ref[c, h]
        st_ref[h] = st

    @pl.when(i == nblk - 1)
    def _():
        for h in range(nh):
            s_ref[h] = st_ref[h].T


def _hgrn_prompt(hq, hk, hv, hlf, tri, lv, levels, L, nh):
    B, Lp, hw = hq.shape
    tm = ROW_TILE
    nb = Lp // tm
    hd = hw // nh
    nchunk = tm // HG_CHUNK
    row = pl.BlockSpec((None, tm, hw), lambda b, i: (b, i, 0))
    return pl.pallas_call(
        functools.partial(_hgrn_prompt_body, L, nh, levels),
        grid=(B, nb),
        in_specs=[row, row, row, row, _full(tri.shape), _full(lv.shape)],
        out_specs=(row, pl.BlockSpec((None, nh, hd, hd), lambda b, i: (b, 0, 0, 0))),
        out_shape=(jax.ShapeDtypeStruct((B, Lp, hw), F32),
                   jax.ShapeDtypeStruct((B, nh, hd, hd), F32)),
        scratch_shapes=[pltpu.VMEM((nh, hd, hd), F32), pltpu.VMEM((tm, hw), BF16),
                        pltpu.VMEM((nchunk, nh, hd, hd), F32), pltpu.VMEM((nchunk, nh, 8, hd), F32)],
        compiler_params=_cparams(2), name="hgrn_prompt",
    )(hq, hk, hv, hlf, tri, lv)


def _merge_body(nh, x_ref, att_ref, ho_ref, gate_ref, ga_ref, gh_ref, gf_ref, w_ref,
                xo_ref, h2_ref):
    att = att_ref[...]
    a = _rms_rows(att, ga_ref[...])
    ho = ho_ref[...]
    hd = ho.shape[1] // nh
    gh = gh_ref[...]
    parts = [a]
    for h in range(nh):
        sl = slice(h * hd, (h + 1) * hd)
        parts.append(_rms_rows(ho[:, sl], gh[:, sl]) * gate_ref[:, sl])
    m = jnp.concatenate(parts, axis=1).astype(BF16)
    xn = x_ref[...] + _dot(m, w_ref[...])
    xo_ref[...] = xn
    h2_ref[...] = _rms_rows(xn, gf_ref[...]).astype(BF16)


def _merge(x, att, ho, gate, ga, gh, gf, w, nh, tm):
    R, D = x.shape
    aw = att.shape[1]
    hw = ho.shape[1]
    row = lambda width: pl.BlockSpec((tm, width), lambda i: (i, 0))
    return pl.pallas_call(
        functools.partial(_merge_body, nh),
        grid=(R // tm,),
        in_specs=[row(D), row(aw), row(hw), row(hw), _full((1, aw)), _full((1, hw)),
                  _full((1, D)), _resident(w.shape)],
        out_specs=(row(D), row(D)),
        out_shape=(jax.ShapeDtypeStruct((R, D), F32), jax.ShapeDtypeStruct((R, D), BF16)),
        compiler_params=_cparams(1), name="merge_out",
    )(x, att, ho, gate, ga, gh, gf, w)


def _convffn_body(seq_len, state_rows, x_ref, h2_ref, wup_ref, cw_ref, cb_ref, wdn_ref, *rest):
    if seq_len is None:
        xo_ref, cs_ref, halo_ref, buf_ref, acc_ref = rest
    else:
        p1_ref, p2_ref, xo_ref, a_ref, halo_ref, buf_ref, acc_ref = rest
    i = pl.program_id(1)
    tm = h2_ref.shape[0]
    dff = wdn_ref.shape[0]
    tn = FF_TILE
    h2 = h2_ref[...]

    @pl.when(i == 0)
    def _():
        halo_ref[...] = jnp.zeros_like(halo_ref)

    if seq_len is not None:
        tmod = lax.broadcasted_iota(jnp.int32, (tm, tn), 0) % seq_len

    for c in range(dff // tn):
        cs = slice(c * tn, (c + 1) * tn)
        a = _dot(h2, wup_ref[:, c * tn:(c + 1) * tn])
        b = _dot(h2, wup_ref[:, dff + c * tn:dff + (c + 1) * tn])
        buf_ref[0:8, :] = halo_ref[:, cs]
        buf_ref[8:tm + 8, :] = a
        halo_ref[:, cs] = a[tm - 8:tm, :]
        a1 = buf_ref[7:tm + 7, :]
        a2 = buf_ref[6:tm + 6, :]
        if seq_len is None:
            lo, blk = state_rows
            @pl.when(i == blk)
            def _():
                cs_ref[:, cs] = a[lo:lo + 2, :]
        else:
            a1 = jnp.where(tmod >= 1, a1, p1_ref[:, cs])
            a2 = jnp.where(tmod >= 2, a2, p2_ref[:, cs])
            a_ref[:, cs] = a
        cv = cb_ref[:, cs] + cw_ref[0:1, cs] * a2 + cw_ref[1:2, cs] * a1 + cw_ref[2:3, cs] * a
        gt = (cv * _sigmoid(cv) * b).astype(BF16)
        y = _dot(gt, wdn_ref[c * tn:(c + 1) * tn, :])
        if c == 0:
            acc_ref[...] = y
        else:
            acc_ref[...] += y
    xo_ref[...] = x_ref[...] + acc_ref[...]


def _convffn_prompt(x, h2, wup, cw, cb, wdn, L):
    B, Lp, D = x.shape
    tm = ROW_TILE
    nb = Lp // tm
    dff = wdn.shape[0]
    row = pl.BlockSpec((None, tm, D), lambda b, i: (b, i, 0))
    state_rows = ((L - 2) % tm, (L - 2) // tm)
    return pl.pallas_call(
        functools.partial(_convffn_body, None, state_rows),
        grid=(B, nb),
        in_specs=[row, row, _resident(wup.shape), _full(cw.shape), _full(cb.shape),
                  _resident(wdn.shape)],
        out_specs=(row, pl.BlockSpec((None, 2, dff), lambda b, i: (b, 0, 0))),
        out_shape=(jax.ShapeDtypeStruct((B, Lp, D), F32), jax.ShapeDtypeStruct((B, 2, dff), F32)),
        scratch_shapes=[pltpu.VMEM((8, dff), F32), pltpu.VMEM((tm + 8, FF_TILE), F32),
                        pltpu.VMEM((tm, D), F32)],
        compiler_params=_cparams(2), name="convffn_prompt",
    )(x, h2, wup, cw, cb, wdn)


def _convffn_sample(x, h2, wup, cw, cb, wdn, p1, p2, seq_len):
    R, D = x.shape
    dff = wdn.shape[0]
    full2 = lambda shape: pl.BlockSpec(shape, lambda b, i: (0, 0))
    return pl.pallas_call(
        functools.partial(_convffn_body, seq_len, None),
        grid=(1, 1),
        in_specs=[full2((R, D)), full2((R, D)), full2(wup.shape), full2(cw.shape),
                  full2(cb.shape), full2(wdn.shape), full2((R, dff)), full2((R, dff))],
        out_specs=(full2((R, D)), full2((R, dff))),
        out_shape=(jax.ShapeDtypeStruct((R, D), F32), jax.ShapeDtypeStruct((R, dff), F32)),
        scratch_shapes=[pltpu.VMEM((8, dff), F32), pltpu.VMEM((R + 8, FF_TILE), F32),
                        pltpu.VMEM((R, D), F32)],
        compiler_params=_cparams(2), name="convffn_sample",
    )(x, h2, wup, cw, cb, wdn, p1, p2)


def _sample_tables(page, ah, npp):
    later_key = np.tril(np.ones((page, page), np.float32), -1)
    jj = np.arange(npp * ah)
    later_page = ((jj[None, :] % ah == jj[:, None] % ah) &
                  (jj[None, :] // ah > jj[:, None] // ah)).astype(np.float32)
    prefix = np.triu(np.ones((NEW_LANES, NEW_LANES), np.float32))
    return later_key, later_page, prefix


def _fox_sample_body(npp, ah, dh, T, pt_ref, q_ref, kn_ref, vn_ref, alft_ref, uf_ref, wx_ref,
                     pre_ref, *rest):
    k_refs = rest[0:npp]
    v_refs = rest[npp:2 * npp]
    lf_refs = rest[2 * npp:3 * npp]
    o_ref, qbd_ref, m_ref, l_ref, acc_ref, tail_ref = rest[3 * npp:]
    g = pl.program_id(1)
    ng = pl.num_programs(1)
    aw = ah * dh
    page = k_refs[0].shape[1]
    R = T * ah

    lane = lax.broadcasted_iota(jnp.int32, (ah, aw), 1)
    head = lax.broadcasted_iota(jnp.int32, (ah, aw), 0)
    hmask = lane // dh == head

    @pl.when(g == 0)
    def _():
        q = q_ref[...] * (float(dh) ** -0.5)
        blocks = [jnp.where(hmask, jnp.broadcast_to(q[t:t + 1, :], (ah, aw)), 0.0)
                  for t in range(T)]
        qbd = jnp.concatenate(blocks, axis=0).astype(BF16)
        qbd_ref[...] = qbd
        a0, a1, a2 = _split3(alft_ref[...])
        pre = pre_ref[...]
        cn = _dot(a0, pre) + _dot(a1, pre) + _dot(a2, pre)
        s = _dot_nt(qbd, kn_ref[...].astype(BF16)) - jnp.concatenate([cn] * T, axis=0)
        col = lax.broadcasted_iota(jnp.int32, s.shape, 1)
        rt = lax.broadcasted_iota(jnp.int32, s.shape, 0) // ah
        s = jnp.where((col <= rt) & (col < T), s, -jnp.inf)
        m = jnp.max(s, axis=1, keepdims=True)
        pr = jnp.exp(s - m)
        m_ref[...] = m
        l_ref[...] = jnp.sum(pr, axis=1, keepdims=True)
        acc_ref[...] = _dot(pr.astype(BF16), vn_ref[...].astype(BF16))
        tail_ref[...] = jnp.zeros_like(tail_ref)

    lft = jnp.concatenate([r[...] for r in lf_refs], axis=0)
    p0, p1, p2 = _split3(lft)
    uf = uf_ref[...]
    wx = wx_ref[...]
    within = _dot(p0, uf) + _dot(p1, uf) + _dot(p2, uf)
    later = jnp.sum(_dot(wx, p0) + _dot(wx, p1) + _dot(wx, p2), axis=1, keepdims=True)
    tail = tail_ref[...]
    bias = within + later + jnp.concatenate([tail] * npp, axis=0)
    tot = jnp.sum(lft, axis=1, keepdims=True)
    step_tot = tot[0:ah]
    for j in range(1, npp):
        step_tot = step_tot + tot[j * ah:(j + 1) * ah]
    tail_ref[...] = tail + step_tot

    qbd = qbd_ref[...]
    s_parts = []
    for j in range(npp):
        kj = k_refs[j][...].astype(BF16)
        bj = bias[j * ah:(j + 1) * ah, :]
        s_parts.append(_dot(qbd, kj) + jnp.concatenate([bj] * T, axis=0))
    s = jnp.concatenate(s_parts, axis=1)
    m_old = m_ref[...]
    m_new = jnp.maximum(m_old, jnp.max(s, axis=1, keepdims=True))
    alpha = jnp.exp(m_old - m_new)
    pr = jnp.exp(s - m_new)
    l_ref[...] = alpha * l_ref[...] + jnp.sum(pr, axis=1, keepdims=True)
    m_ref[...] = m_new
    prb = pr.astype(BF16)
    acc = alpha * acc_ref[...]
    for j in range(npp):
        acc = acc + _dot_nt(prb[:, j * page:(j + 1) * page], v_refs[j][...].astype(BF16))
    acc_ref[...] = acc

    @pl.when(g == ng - 1)
    def _():
        out = acc / l_ref[...]
        rows = []
        for t in range(T):
            blk = jnp.where(hmask, out[t * ah:(t + 1) * ah, :], 0.0)
            rows.append(jnp.sum(blk, axis=0, keepdims=True))
        o_ref[...] = jnp.concatenate(rows, axis=0)


def _fox_sample(layer, page_table, q, kn, vn, alft, cache_kT, cache_vT, cache_lfT, uf, wx, pre, ah):
    Bd, T, aw = q.shape
    dh = aw // ah
    n_pages = page_table.shape[1]
    page = cache_kT.shape[3]
    npp = min(PAGES_PER_STEP, n_pages)
    ng = n_pages // npp
    R = T * ah

    def page_map(r):
        return lambda b, g, pt: (layer, pt[b, n_pages - (g + 1) * npp + r], 0, 0)

    seq = lambda shape: pl.BlockSpec((None,) + shape, lambda b, g, pt: (b, 0, 0))
    const = lambda shape: pl.BlockSpec(shape, lambda b, g, pt: (0,) * len(shape))
    in_specs = [seq((T, aw)), seq((NEW_LANES, aw)), seq((NEW_LANES, aw)), seq((ah, NEW_LANES)),
                const(uf.shape), const(wx.shape), const(pre.shape)]
    in_specs += [pl.BlockSpec((None, None, aw, page), page_map(r)) for r in range(npp)]
    in_specs += [pl.BlockSpec((None, None, aw, page), page_map(r)) for r in range(npp)]
    in_specs += [pl.BlockSpec((None, None, ah, page), page_map(r)) for r in range(npp)]
    grid_spec = pltpu.PrefetchScalarGridSpec(
        num_scalar_prefetch=1, grid=(Bd, ng), in_specs=in_specs,
        out_specs=pl.BlockSpec((None, T, aw), lambda b, g, pt: (b, 0, 0)),
        scratch_shapes=[pltpu.VMEM((R, aw), BF16), pltpu.VMEM((R, 1), F32),
                        pltpu.VMEM((R, 1), F32), pltpu.VMEM((R, aw), F32),
                        pltpu.VMEM((ah, 1), F32)])
    return pl.pallas_call(
        functools.partial(_fox_sample_body, npp, ah, dh, T),
        grid_spec=grid_spec, out_shape=jax.ShapeDtypeStruct((Bd, T, aw), F32),
        compiler_params=_cparams(2), name="fox_sample",
    )(page_table, q, kn, vn, alft, uf, wx, pre, *([cache_kT] * npp), *([cache_vT] * npp),
      *([cache_lfT] * npp))


def _hgrn_sample_body(nh, T, q_ref, k_ref, v_ref, lf_ref, s0_ref, o_ref, s1_ref):
    hd = s0_ref.shape[-1]
    outs = [[] for _ in range(T)]
    for h in range(nh):
        sl = slice(h * hd, (h + 1) * hd)
        q = [q_ref[t:t + 1, sl] for t in range(T)]
        k = [k_ref[t:t + 1, sl] for t in range(T)]
        v = [v_ref[t:t + 1, sl] for t in range(T)]
        cum = [lf_ref[0:1, sl]]
        for t in range(1, T):
            cum.append(cum[-1] + lf_ref[t:t + 1, sl])
        s0 = s0_ref[h]
        pad = [jnp.zeros((1, hd), F32)] * (8 - T)
        qh = jnp.concatenate([q[t] * jnp.exp(cum[t]) for t in range(T)] + pad, axis=0)
        o_inter = _dot(qh.astype(BF16), s0.astype(BF16))
        for t in range(T):
            o = o_inter[t:t + 1, :]
            for s in range(t + 1):
                a = jnp.sum(q[t] * k[s] * jnp.exp(cum[t] - cum[s]), axis=1, keepdims=True)
                o = o + a * v[s]
            outs[t].append(o)
        kh = jnp.concatenate([k[s] * jnp.exp(cum[T - 1] - cum[s]) for s in range(T)] + pad, axis=0)
        vv = jnp.concatenate(v + pad, axis=0)
        decay_col = jnp.broadcast_to(jnp.exp(cum[T - 1]), (hd, hd)).T
        s1_ref[h] = decay_col * s0 + _dot_tn(kh.astype(BF16), vv.astype(BF16))
    for t in range(T):
        o_ref[t:t + 1, :] = jnp.concatenate(outs[t], axis=1)


def _hgrn_sample(hq, hk, hv, hlf, s0, nh):
    Bd, T, hw = hq.shape
    hd = hw // nh
    seq = pl.BlockSpec((None, T, hw), lambda b: (b, 0, 0))
    st = pl.BlockSpec((None, nh, hd, hd), lambda b: (b, 0, 0, 0))
    return pl.pallas_call(
        functools.partial(_hgrn_sample_body, nh, T),
        grid=(Bd,), in_specs=[seq, seq, seq, seq, st], out_specs=(seq, st),
        out_shape=(jax.ShapeDtypeStruct((Bd, T, hw), F32),
                   jax.ShapeDtypeStruct((Bd, nh, hd, hd), F32)),
        compiler_params=_cparams(1), name="hgrn_sample",
    )(hq, hk, hv, hlf, s0)


def _final_norm_body(x_ref, g_ref, o_ref):
    o_ref[...] = _rms_rows(x_ref[...], g_ref[...])


def _final_norm_shifted_body(x_ref, g_ref, o_ref):
    o_ref[...] = _rms_rows(x_ref[0], g_ref[...])


def _final_norm_prompt(x, g, seq, tm):
    B, _, D = x.shape
    return pl.pallas_call(
        _final_norm_shifted_body, grid=(B, seq // tm),
        in_specs=[pl.BlockSpec((pl.Element(1), pl.Element(tm), pl.Element(D)),
                               lambda b, i: (b, pl.multiple_of(N_META + i * tm, 8), 0)),
                  _full((1, D))],
        out_specs=pl.BlockSpec((None, tm, D), lambda b, i: (b, i, 0)),
        out_shape=jax.ShapeDtypeStruct((B, seq, D), F32),
        compiler_params=_cparams(2), name="final_norm_prompt",
    )(x, g)


def _final_norm_sample(x, g):
    return pl.pallas_call(
        _final_norm_body, grid=(1,), in_specs=[_full(x.shape), _full(g.shape)],
        out_specs=_full(x.shape), out_shape=jax.ShapeDtypeStruct(x.shape, F32),
        compiler_params=_cparams(1), name="final_norm_sample",
    )(x, g)


def _expand_heads(vec, ah):
    rep = jnp.repeat(vec.astype(F32), BIAS_STRIDE)
    return jnp.pad(rep, (0, BIAS_LANES - ah * BIAS_STRIDE))[None, :]


def kernel(x_prompt, x_sample, cache_k, cache_v, cache_logf, state_hgrn, state_conv, page_table,
           meta_tokens, g_mix, w_in, b_f, hg_lb, g_att_out, g_hg_out, w_out, g_ffn, w_up,
           conv_w, conv_b, w_down, g_final):
    B, seq, D = x_prompt.shape
    Bd, T, _ = x_sample.shape
    depth = w_in.shape[0]
    ah = b_f.shape[1]
    aw = cache_k.shape[3] * cache_k.shape[4]
    hw = hg_lb.shape[1]
    nh = state_hgrn.shape[2]
    dff = w_down.shape[1]
    n_pool, page = cache_k.shape[1], cache_k.shape[2]
    L = N_META + seq
    tm = ROW_TILE
    Lp = -(-L // tm) * tm
    assert ah * BIAS_STRIDE <= BIAS_LANES and ah % 2 == 0 and dff % FF_TILE == 0
    assert page_table.shape[1] % min(PAGES_PER_STEP, page_table.shape[1]) == 0 and T <= 8

    sm = jax.nn.softmax(hg_lb.astype(F32), axis=0)
    lb_all = jnp.cumsum(sm, axis=0) - sm[0:1]

    tri = jnp.asarray(np.tril(np.ones((tm, tm), np.float32)), BF16)
    tri_c = jnp.asarray(np.tril(np.ones((HG_CHUNK, HG_CHUNK), np.float32)), BF16)
    lv_np, levels = _hgrn_tables()
    lv = jnp.asarray(lv_np, jnp.int32)
    npp = min(PAGES_PER_STEP, page_table.shape[1])
    uf, wx, pre = (jnp.asarray(a, BF16) for a in _sample_tables(page, ah, npp))

    dh = aw // ah
    cache_kT = cache_k.transpose(0, 1, 3, 4, 2).reshape(depth, n_pool, aw, page)
    cache_vT = cache_v.transpose(0, 1, 3, 4, 2).reshape(depth, n_pool, aw, page)
    cache_lfT = cache_logf.transpose(0, 1, 3, 2)

    xp = jnp.concatenate([jnp.broadcast_to(meta_tokens[None].astype(F32), (B, N_META, D)),
                          x_prompt, jnp.zeros((B, Lp - L, D), F32)], axis=1)
    xs = x_sample.reshape(Bd * T, D)

    outs = [[] for _ in range(10)]
    for l in range(depth):
        wi = w_in[l]
        o = 3 * aw + ah
        zf_cols = wi[:, 3 * aw:o]
        w_arr = jnp.concatenate(
            [wi[:, :3 * aw], wi[:, o:o + 4 * hw],
             jnp.pad(jnp.repeat(zf_cols, BIAS_STRIDE, axis=1),
                     ((0, 0), (0, BIAS_LANES - ah * BIAS_STRIDE))),
             jnp.pad(zf_cols, ((0, 0), (0, BIAS_LANES - ah)))], axis=1).astype(BF16)
        bfe = _expand_heads(b_f[l], ah)
        bfp = jnp.pad(b_f[l].astype(F32), (0, BIAS_LANES - ah))[None, :]
        lb = lb_all[l][None, :]
        g1 = g_mix[l][None, :].astype(F32)
        ga = g_att_out[l][None, :].astype(F32)
        gh = g_hg_out[l][None, :].astype(F32)
        gf = g_ffn[l][None, :].astype(F32)
        wo = w_out[l].astype(BF16)
        wu = w_up[l].astype(BF16)
        wd = w_down[l].astype(BF16)
        cw = conv_w[l].astype(F32)
        cb = conv_b[l][None, :].astype(F32)

        (qT, ktm, kb, vT, k_new, v_new, alf, hq, hk, hv, hlf, gate) = _in_proj_prompt(
            xp, g1, w_arr, bfe, bfp, lb, tri, L, aw, hw, ah, nh)
        att = _fox_prompt(qT, ktm, kb, vT)
        ho, s_p = _hgrn_prompt(hq, hk, hv, hlf, tri_c, lv, levels, L, nh)
        xp2, h2 = _merge(xp.reshape(B * Lp, D), att.reshape(B * Lp, aw), ho.reshape(B * Lp, hw),
                         gate.reshape(B * Lp, hw), ga, gh, gf, wo, nh, tm)
        xp, c_p = _convffn_prompt(xp2.reshape(B, Lp, D), h2.reshape(B, Lp, D), wu, cw, cb, wd, L)
        outs[0].append(k_new.reshape(B, L, ah, aw // ah))
        outs[1].append(v_new.reshape(B, L, ah, aw // ah))
        outs[2].append(alf)
        outs[3].append(s_p)
        outs[4].append(c_p)

        (sq, sk, sv, salf, shq, shk, shv, shlf, sgate) = _in_proj_sample(
            xs, g1, w_arr, bfe, bfp, lb, aw, hw, ah, nh)
        padn = ((0, 0), (0, NEW_LANES - T), (0, 0))
        kn = jnp.pad(sk.reshape(Bd, T, aw), padn)
        vn = jnp.pad(sv.reshape(Bd, T, aw), padn)
        alft = jnp.pad(salf.reshape(Bd, T, ah).transpose(0, 2, 1), ((0, 0), (0, 0), (0, NEW_LANES - T)))
        att_s = _fox_sample(l, page_table, sq.reshape(Bd, T, aw), kn, vn, alft,
                            cache_kT, cache_vT, cache_lfT, uf, wx, pre, ah)
        ho_s, s_s = _hgrn_sample(shq.reshape(Bd, T, hw), shk.reshape(Bd, T, hw),
                                 shv.reshape(Bd, T, hw), shlf.reshape(Bd, T, hw),
                                 state_hgrn[l].astype(F32), nh)
        xs2, h2s = _merge(xs, att_s.reshape(Bd * T, aw), ho_s.reshape(Bd * T, hw), sgate,
                          ga, gh, gf, wo, nh, Bd * T)
        prev = state_conv[l].astype(F32)
        zeros = jnp.zeros((Bd, T - 1, dff), F32)
        p1 = jnp.concatenate([prev[:, 1:2], zeros], axis=1).reshape(Bd * T, dff)
        p2 = jnp.concatenate([prev, zeros[:, :T - 2]], axis=1).reshape(Bd * T, dff)
        xs, a_s = _convffn_sample(xs2, h2s, wu, cw, cb, wd, p1, p2, T)
        outs[5].append(sk.reshape(Bd, T, ah, aw // ah))
        outs[6].append(sv.reshape(Bd, T, ah, aw // ah))
        outs[7].append(salf.reshape(Bd, T, ah))
        outs[8].append(s_s)
        outs[9].append(a_s.reshape(Bd, T, dff)[:, T - 2:])

    gfin = g_final[None, :].astype(F32)
    y_prompt = _final_norm_prompt(xp, gfin, seq, 512 if seq % 512 == 0 else seq)
    y_sample = _final_norm_sample(xs, gfin).reshape(Bd, T, D)
    return (y_prompt, y_sample) + tuple(jnp.stack(o) for o in outs)
```

```python
import functools

import numpy as np
import jax
import jax.numpy as jnp
from jax import lax
from jax.experimental import pallas as pl
from jax.experimental.pallas import tpu as pltpu

F32 = jnp.float32
BF16 = jnp.bfloat16

N_META = 16
EPS = 1e-6
LOG2E = 1.4426950408889634
ROW_TILE = 640
HG_CHUNK = 128
FF_TILE = 256
FOX_HEADS_PER_STEP = 4
BIAS_LANES = 128
BIAS_STRIDE = 16
PAGES_PER_STEP = 16
NEW_LANES = 128
VMEM_LIMIT = 56 * 1024 * 1024


def _cparams(n_axes):
    return pltpu.CompilerParams(dimension_semantics=("arbitrary",) * n_axes,
                                vmem_limit_bytes=VMEM_LIMIT)


def _full(shape):
    nd = len(shape)
    return pl.BlockSpec(shape, lambda *_: (0,) * nd)


def _resident(shape):
    nd = len(shape)
    return pl.BlockSpec(shape, lambda *_: (0,) * nd, pipeline_mode=pl.Buffered(1))


def _split3(x):
    hi = x.astype(BF16)
    r1 = x - hi.astype(F32)
    mid = r1.astype(BF16)
    lo = (r1 - mid.astype(F32)).astype(BF16)
    return hi, mid, lo


def _log_sigmoid(x):
    return jnp.minimum(x, 0.0) - jnp.log1p(jnp.exp(-jnp.abs(x)))


def _sigmoid(x):
    return 1.0 / (1.0 + jnp.exp(-x))


def _silu(x):
    return x * (0.5 * jnp.tanh(0.5 * x) + 0.5)


def _rms_rows(x, g):
    return x * lax.rsqrt(jnp.mean(x * x, axis=-1, keepdims=True) + EPS) * g


def _dot(a, b):
    return jnp.dot(a, b, preferred_element_type=F32)


def _dot_nt(a, b):
    return lax.dot_general(a, b, (((1,), (1,)), ((), ())), preferred_element_type=F32)


def _dot_tn(a, b):
    return lax.dot_general(a, b, (((0,), (0,)), ((), ())), preferred_element_type=F32)


def _in_proj_body(prompt, aw, hw, ah, dh, hd, x_ref, g_ref, w_ref, bfe_ref, bfp_ref, lb_ref, *rest):
    if prompt:
        (tri_ref, qT_ref, ktm_ref, kb_ref, vT_ref, ko_ref, vo_ref, alf_ref,
         hq_ref, hk_ref, hv_ref, hlf_ref, gate_ref, carry_ref) = rest
    else:
        (q_ref, ko_ref, vo_ref, alf_ref,
         hq_ref, hk_ref, hv_ref, hlf_ref, gate_ref) = rest

    hb = _rms_rows(x_ref[...], g_ref[...]).astype(BF16)

    def col(c0, n):
        return _dot(hb, w_ref[:, c0:c0 + n])

    q = col(0, aw)
    k = col(aw, aw)
    v = col(2 * aw, aw)
    o = 3 * aw
    hq = col(o, hw)
    hf = col(o + hw, hw)
    hi = col(o + 2 * hw, hw)
    hg = col(o + 3 * hw, hw)
    zfp = col(o + 4 * hw + BIAS_LANES, BIAS_LANES)

    ko_ref[...] = k
    vo_ref[...] = v
    alfp = _log_sigmoid(zfp + bfp_ref[...])
    alf_ref[...] = alfp[:, :ah]

    lb = lb_ref[...]
    f = lb + (1.0 - lb) * _sigmoid(hf)
    hq_ref[...] = hq * (float(hd) ** -0.5)
    hk_ref[...] = 1.0 - f
    hv_ref[...] = hi
    hlf_ref[...] = jnp.log(f)
    gate_ref[...] = _silu(hg)

    if not prompt:
        q_ref[...] = q
        return

    alfe = _log_sigmoid(col(o + 4 * hw, BIAS_LANES) + bfe_ref[...])
    i = pl.program_id(1)

    @pl.when(i == 0)
    def _():
        carry_ref[...] = jnp.zeros_like(carry_ref)

    tri = tri_ref[...]
    p0, p1, p2 = _split3(alfe)
    cum = _dot(tri, p0) + _dot(tri, p1) + _dot(tri, p2) + carry_ref[...]
    tm = cum.shape[0]
    carry_ref[...] = cum[tm - 1:tm, :]
    n0, n1, n2 = _split3(cum * (-LOG2E))
    lane = lax.broadcasted_iota(jnp.int32, cum.shape, 1) % BIAS_STRIDE
    zero = jnp.zeros_like(n0)
    kb_ref[...] = jnp.where(lane == 0, n0, jnp.where(lane == 1, n1, jnp.where(lane == 2, n2, zero)))

    qT_ref[...] = (q * (float(dh) ** -0.5 * LOG2E)).T.astype(BF16)
    ktm_ref[...] = k.astype(BF16)
    vt = v.T
    for h in range(ah):
        vT_ref[h] = vt[h * dh:(h + 1) * dh, :].astype(BF16)


def _in_proj_prompt(x, g, w, bfe, bfp, lb, tri, L, aw, hw, ah, nh):
    B, Lp, D = x.shape
    tm = ROW_TILE
    nb = Lp // tm
    dh = aw // ah
    nc = w.shape[1]
    row = lambda width: pl.BlockSpec((None, tm, width), lambda b, i: (b, i, 0))
    out_shape = (
        jax.ShapeDtypeStruct((B, nb, aw, tm), BF16),
        jax.ShapeDtypeStruct((B, Lp, aw), BF16),
        jax.ShapeDtypeStruct((B, Lp, BIAS_LANES), BF16),
        jax.ShapeDtypeStruct((B, ah, nb, dh, tm), BF16),
        jax.ShapeDtypeStruct((B, L, aw), F32),
        jax.ShapeDtypeStruct((B, L, aw), F32),
        jax.ShapeDtypeStruct((B, L, ah), F32),
        jax.ShapeDtypeStruct((B, Lp, hw), F32),
        jax.ShapeDtypeStruct((B, Lp, hw), F32),
        jax.ShapeDtypeStruct((B, Lp, hw), F32),
        jax.ShapeDtypeStruct((B, Lp, hw), F32),
        jax.ShapeDtypeStruct((B, Lp, hw), F32),
    )
    out_specs = (
        pl.BlockSpec((None, None, aw, tm), lambda b, i: (b, i, 0, 0)),
        row(aw), row(BIAS_LANES),
        pl.BlockSpec((None, ah, None, dh, tm), lambda b, i: (b, 0, i, 0, 0)),
        row(aw), row(aw), row(ah),
        row(hw), row(hw), row(hw), row(hw), row(hw),
    )
    in_specs = [row(D), _full((1, D)), _resident((D, nc)), _full((1, BIAS_LANES)),
                _full((1, BIAS_LANES)), _full((1, hw)), _resident((tm, tm))]
    return pl.pallas_call(
        functools.partial(_in_proj_body, True, aw, hw, ah, dh, hw // nh),
        grid=(B, nb), in_specs=in_specs, out_specs=out_specs, out_shape=out_shape,
        scratch_shapes=[pltpu.VMEM((1, BIAS_LANES), F32)],
        compiler_params=_cparams(2), name="in_proj_prompt",
    )(x, g, w, bfe, bfp, lb, tri)


def _in_proj_sample(x, g, w, bfe, bfp, lb, aw, hw, ah, nh):
    R, D = x.shape
    dh = aw // ah
    nc = w.shape[1]
    out_shape = (
        jax.ShapeDtypeStruct((R, aw), F32),
        jax.ShapeDtypeStruct((R, aw), F32),
        jax.ShapeDtypeStruct((R, aw), F32),
        jax.ShapeDtypeStruct((R, ah), F32),
        jax.ShapeDtypeStruct((R, hw), F32), jax.ShapeDtypeStruct((R, hw), F32),
        jax.ShapeDtypeStruct((R, hw), F32), jax.ShapeDtypeStruct((R, hw), F32),
        jax.ShapeDtypeStruct((R, hw), F32),
    )
    in_specs = [_full((R, D)), _full((1, D)), _full((D, nc)), _full((1, BIAS_LANES)),
                _full((1, BIAS_LANES)), _full((1, hw))]
    out_specs = tuple(_full(s.shape) for s in out_shape)
    return pl.pallas_call(
        functools.partial(_in_proj_body, False, aw, hw, ah, dh, hw // nh),
        grid=(1,), in_specs=in_specs, out_specs=out_specs, out_shape=out_shape,
        compiler_params=_cparams(1), name="in_proj_sample",
    )(x, g, w, bfe, bfp, lb)


def _fox_prompt_body(dh, hps, qT_ref, k_ref, kb_ref, vT_ref, o_ref, rhs_ref):
    p = pl.program_id(1)
    qi = pl.program_id(2)
    tq = qT_ref.shape[1]
    tk = tq
    pw = 2 * dh

    row = lax.broadcasted_iota(jnp.int32, (pw, tq), 0)
    brow = lax.broadcasted_iota(jnp.int32, (BIAS_LANES, tq), 0)
    for e in range(hps):
        h = hps * p + e
        pr0 = (e // 2) * pw
        qT = qT_ref[pr0:pr0 + pw, :]
        rhs_ref[e, 0:pw, :] = jnp.where(row // dh == e % 2, qT, jnp.zeros_like(qT))
        ones = (brow // BIAS_STRIDE == h) & (brow % BIAS_STRIDE < 3)
        rhs_ref[e, pw:pw + BIAS_LANES, :] = jnp.where(ones, 1.0, 0.0).astype(BF16)

    def scores(ki):
        r0 = pl.multiple_of(ki * tk, tk)
        kb = kb_ref[pl.ds(r0, tk), :]
        out = []
        for e in range(hps):
            pr0 = (e // 2) * pw
            lhs = jnp.concatenate([k_ref[pl.ds(r0, tk), pr0:pr0 + pw], kb], axis=1)
            out.append(_dot(lhs, rhs_ref[e]))
        return out

    def update(e, ki, s, m, l, acc):
        m_new = jnp.maximum(m, jnp.max(s, axis=0, keepdims=True))
        alpha = jnp.exp2(m - m_new)
        pr = jnp.exp2(s - m_new)
        l = alpha * l + jnp.sum(pr, axis=0, keepdims=True)
        acc = alpha * acc + _dot(vT_ref[e, ki], pr.astype(BF16))
        return m_new, l, acc

    def step(ki, carry):
        ss = scores(ki)
        out = []
        for e in range(hps):
            out.extend(update(e, ki, ss[e], *carry[3 * e:3 * e + 3]))
        return tuple(out)

    init = []
    for e in range(hps):
        init.extend([jnp.full((1, tq), -jnp.inf, F32), jnp.zeros((1, tq), F32),
                     jnp.zeros((dh, tq), F32)])
    carry = lax.fori_loop(0, qi, step, tuple(init))

    ss = scores(qi)
    kpos = lax.broadcasted_iota(jnp.int32, (tk, tq), 0)
    qpos = lax.broadcasted_iota(jnp.int32, (tk, tq), 1)
    outs = []
    for e in range(hps):
        s = jnp.where(kpos <= qpos, ss[e], -jnp.inf)
        m, l, acc = update(e, qi, s, *carry[3 * e:3 * e + 3])
        outs.append(acc / l)
    for pr_i in range(hps // 2):
        o_ref[:, pr_i * pw:(pr_i + 1) * pw] = jnp.concatenate(outs[2 * pr_i:2 * pr_i + 2], axis=0).T


def _fox_prompt(qT, ktm, kb, vT):
    B, nb, aw, tq = qT.shape
    _, ah, _, dh, _ = vT.shape
    Lp = ktm.shape[1]
    hps = FOX_HEADS_PER_STEP if ah % FOX_HEADS_PER_STEP == 0 else 2
    in_specs = [
        pl.BlockSpec((None, None, hps * dh, tq), lambda b, p, i: (b, i, p, 0)),
        pl.BlockSpec((None, Lp, hps * dh), lambda b, p, i: (b, 0, p)),
        pl.BlockSpec((None, Lp, BIAS_LANES), lambda b, p, i: (b, 0, 0)),
        pl.BlockSpec((None, hps, nb, dh, tq), lambda b, p, i: (b, p, 0, 0, 0)),
    ]
    return pl.pallas_call(
        functools.partial(_fox_prompt_body, dh, hps),
        grid=(B, ah // hps, nb), in_specs=in_specs,
        out_specs=pl.BlockSpec((None, tq, hps * dh), lambda b, p, i: (b, i, p)),
        out_shape=jax.ShapeDtypeStruct((B, Lp, aw), F32),
        scratch_shapes=[pltpu.VMEM((hps, 2 * dh + BIAS_LANES, tq), BF16)],
        compiler_params=_cparams(3), name="fox_prompt",
    )(qT, ktm, kb, vT)


def _hgrn_tables():
    C = HG_CHUNK
    levels = []
    m = C // 2
    while m >= 1:
        levels.append(m)
        m //= 2
    LV = -np.ones((C, C), np.int32)
    for li, m in enumerate(levels):
        for t in range(C):
            for s in range(t):
                if t // (2 * m) == s // (2 * m) and t % (2 * m) >= m and s % (2 * m) < m:
                    LV[t, s] = li
    for t in range(C):
        LV[t, t] = len(levels)
    return LV, levels


def _level_exponent(cum, lf, m, rowi):
    C, W = cum.shape
    if m >= 4:
        g = cum.reshape(C // (2 * m), 2 * m, W)
        mid = jnp.broadcast_to(g[:, m - 1:m, :], g.shape).reshape(C, W)
        return jnp.where((rowi & m) != 0, cum - mid, mid - cum)
    if m == 2:
        nxt = pltpu.roll(lf, C - 1, 0)
        prv = pltpu.roll(lf, 1, 0)
        r4 = rowi & 3
        return jnp.where(r4 == 0, nxt, jnp.where(r4 == 1, 0.0, jnp.where(r4 == 2, lf, prv + lf)))
    return jnp.where((rowi & 1) != 0, lf, 0.0)


def _hgrn_prompt_body(L, nh, levels, q_ref, k_ref, v_ref, lf_ref, tri_ref, lv_ref,
                      o_ref, s_ref, st_ref, qh_ref, u_ref, dec_ref):
    i = pl.program_id(1)
    nblk = pl.num_programs(1)
    tm = q_ref.shape[0]
    C = HG_CHUNK
    hd = q_ref.shape[1] // nh
    nchunk = tm // C

    @pl.when(i == 0)
    def _():
        st_ref[...] = jnp.zeros_like(st_ref)

    tri = tri_ref[...]
    lv = lv_ref[...]
    rowi = lax.broadcasted_iota(jnp.int32, (C, hd), 0)

    def chunk(c, carry):
        r0 = pl.multiple_of(c * C, C)
        valid = (i * tm + r0 + rowi) < L
        for h in range(nh):
            sl = slice(h * hd, (h + 1) * hd)
            q = q_ref[pl.ds(r0, C), sl]
            k = k_ref[pl.ds(r0, C), sl]
            v = jnp.where(valid, v_ref[pl.ds(r0, C), sl], 0.0)
            lf = jnp.where(valid, lf_ref[pl.ds(r0, C), sl], 0.0)
            l0, l1, l2 = _split3(lf)
            cum = _dot(tri, l0) + _dot(tri, l1) + _dot(tri, l2)
            last = cum[C - 1:C, :]
            qh_ref[pl.ds(r0, C), sl] = (q * jnp.exp(cum)).astype(BF16)
            kh = (k * jnp.exp(last - cum)).astype(BF16)
            vb = v.astype(BF16)
            u_ref[c, h] = _dot_tn(vb, kh)
            dec_ref[c, h] = jnp.broadcast_to(jnp.exp(last), (8, hd))
            a = jnp.where(lv == len(levels), _dot_nt(q.astype(BF16), k.astype(BF16)), 0.0)
            for li, m in enumerate(levels):
                e = _level_exponent(cum, lf, m, rowi)
                w = (jnp.where((rowi & m) != 0, q, k) * jnp.exp(e)).astype(BF16)
                a = a + jnp.where(lv == li, _dot_nt(w, w), 0.0)
            o_ref[pl.ds(r0, C), sl] = _dot(a.astype(BF16), vb)
        return carry

    lax.fori_loop(0, nchunk, chunk, 0, unroll=2)

    for h in range(nh):
        sl = slice(h * hd, (h + 1) * hd)
        st = st_ref[h]
        for c in range(nchunk):
            rows = slice(c * C, (c + 1) * C)
            o_ref[rows, sl] += _dot_nt(qh_ref[rows, sl], st.astype(BF16))
            st = st * dec_ref[c, h][0:1, :] + u_ref[c, h]
        st_ref[h] = st

    @pl.when(i == nblk - 1)
    def _():
        for h in range(nh):
            s_ref[h] = st_ref[h].T


def _hgrn_prompt(hq, hk, hv, hlf, tri, lv, levels, L, nh):
    B, Lp, hw = hq.shape
    tm = ROW_TILE
    nb = Lp // tm
    hd = hw // nh
    nchunk = tm // HG_CHUNK
    row = pl.BlockSpec((None, tm, hw), lambda b, i: (b, i, 0))
    return pl.pallas_call(
        functools.partial(_hgrn_prompt_body, L, nh, levels),
        grid=(B, nb),
        in_specs=[row, row, row, row, _full(tri.shape), _full(lv.shape)],
        out_specs=(row, pl.BlockSpec((None, nh, hd, hd), lambda b, i: (b, 0, 0, 0))),
        out_shape=(jax.ShapeDtypeStruct((B, Lp, hw), F32),
                   jax.ShapeDtypeStruct((B, nh, hd, hd), F32)),
        scratch_shapes=[pltpu.VMEM((nh, hd, hd), F32), pltpu.VMEM((tm, hw), BF16),
                        pltpu.VMEM((nchunk, nh, hd, hd), F32), pltpu.VMEM((nchunk, nh, 8, hd), F32)],
        compiler_params=_cparams(2), name="hgrn_prompt",
    )(hq, hk, hv, hlf, tri, lv)


def _merge_body(nh, x_ref, att_ref, ho_ref, gate_ref, ga_ref, gh_ref, gf_ref, w_ref,
                xo_ref, h2_ref):
    att = att_ref[...]
    a = _rms_rows(att, ga_ref[...])
    ho = ho_ref[...]
    hd = ho.shape[1] // nh
    gh = gh_ref[...]
    parts = [a]
    for h in range(nh):
        sl = slice(h * hd, (h + 1) * hd)
        parts.append(_rms_rows(ho[:, sl], gh[:, sl]) * gate_ref[:, sl])
    m = jnp.concatenate(parts, axis=1).astype(BF16)
    xn = x_ref[...] + _dot(m, w_ref[...])
    xo_ref[...] = xn
    h2_ref[...] = _rms_rows(xn, gf_ref[...]).astype(BF16)


def _merge(x, att, ho, gate, ga, gh, gf, w, nh, tm):
    R, D = x.shape
    aw = att.shape[1]
    hw = ho.shape[1]
    row = lambda width: pl.BlockSpec((tm, width), lambda i: (i, 0))
    return pl.pallas_call(
        functools.partial(_merge_body, nh),
        grid=(R // tm,),
        in_specs=[row(D), row(aw), row(hw), row(hw), _full((1, aw)), _full((1, hw)),
                  _full((1, D)), _resident(w.shape)],
        out_specs=(row(D), row(D)),
        out_shape=(jax.ShapeDtypeStruct((R, D), F32), jax.ShapeDtypeStruct((R, D), BF16)),
        compiler_params=_cparams(1), name="merge_out",
    )(x, att, ho, gate, ga, gh, gf, w)


def _convffn_body(seq_len, state_rows, x_ref, h2_ref, wup_ref, cw_ref, cb_ref, wdn_ref, *rest):
    if seq_len is None:
        xo_ref, cs_ref, halo_ref, gt_ref = rest
    else:
        p1_ref, p2_ref, xo_ref, a_ref, halo_ref, gt_ref = rest
    i = pl.program_id(1)
    tm = h2_ref.shape[0]
    dff = wdn_ref.shape[0]
    tn = FF_TILE
    h2 = h2_ref[...]

    @pl.when(i == 0)
    def _():
        halo_ref[...] = jnp.zeros_like(halo_ref)

    row = lax.broadcasted_iota(jnp.int32, (tm, tn), 0)
    if seq_len is not None:
        tmod = row % seq_len

    for c in range(dff // tn):
        cs = slice(c * tn, (c + 1) * tn)
        a = _dot(h2, wup_ref[:, c * tn:(c + 1) * tn])
        b = _dot(h2, wup_ref[:, dff + c * tn:dff + (c + 1) * tn])
        prev1 = halo_ref[7:8, cs]
        prev2 = halo_ref[6:7, cs]
        a1 = jnp.where(row == 0, prev1, pltpu.roll(a, 1, 0))
        a2 = jnp.where(row == 0, prev2, jnp.where(row == 1, prev1, pltpu.roll(a, 2, 0)))
        halo_ref[:, cs] = a[tm - 8:tm, :]
        if seq_len is None:
            lo, blk = state_rows
            @pl.when(i == blk)
            def _():
                cs_ref[:, cs] = a[lo:lo + 2, :]
        else:
            a1 = jnp.where(tmod >= 1, a1, p1_ref[:, cs])
            a2 = jnp.where(tmod >= 2, a2, p2_ref[:, cs])
            a_ref[:, cs] = a
        cv = cb_ref[:, cs] + cw_ref[0:1, cs] * a2 + cw_ref[1:2, cs] * a1 + cw_ref[2:3, cs] * a
        gt_ref[:, cs] = (_silu(cv) * b).astype(BF16)
    xo_ref[...] = x_ref[...] + _dot(gt_ref[...], wdn_ref[...])


def _convffn_prompt(x, h2, wup, cw, cb, wdn, L):
    B, Lp, D = x.shape
    tm = ROW_TILE
    nb = Lp // tm
    dff = wdn.shape[0]
    row = pl.BlockSpec((None, tm, D), lambda b, i: (b, i, 0))
    state_rows = ((L - 2) % tm, (L - 2) // tm)
    return pl.pallas_call(
        functools.partial(_convffn_body, None, state_rows),
        grid=(B, nb),
        in_specs=[row, row, _resident(wup.shape), _full(cw.shape), _full(cb.shape),
                  _resident(wdn.shape)],
        out_specs=(row, pl.BlockSpec((None, 2, dff), lambda b, i: (b, 0, 0))),
        out_shape=(jax.ShapeDtypeStruct((B, Lp, D), F32), jax.ShapeDtypeStruct((B, 2, dff), F32)),
        scratch_shapes=[pltpu.VMEM((8, dff), F32), pltpu.VMEM((tm, dff), BF16)],
        compiler_params=_cparams(2), name="convffn_prompt",
    )(x, h2, wup, cw, cb, wdn)


def _convffn_sample(x, h2, wup, cw, cb, wdn, p1, p2, seq_len):
    R, D = x.shape
    dff = wdn.shape[0]
    full2 = lambda shape: pl.BlockSpec(shape, lambda b, i: (0, 0))
    return pl.pallas_call(
        functools.partial(_convffn_body, seq_len, None),
        grid=(1, 1),
        in_specs=[full2((R, D)), full2((R, D)), full2(wup.shape), full2(cw.shape),
                  full2(cb.shape), full2(wdn.shape), full2((R, dff)), full2((R, dff))],
        out_specs=(full2((R, D)), full2((R, dff))),
        out_shape=(jax.ShapeDtypeStruct((R, D), F32), jax.ShapeDtypeStruct((R, dff), F32)),
        scratch_shapes=[pltpu.VMEM((8, dff), F32), pltpu.VMEM((R, dff), BF16)],
        compiler_params=_cparams(2), name="convffn_sample",
    )(x, h2, wup, cw, cb, wdn, p1, p2)


def _sample_tables(page, ah, npp):
    later_key = np.tril(np.ones((page, page), np.float32), -1)
    jj = np.arange(npp * ah)
    later_page = ((jj[None, :] % ah == jj[:, None] % ah) &
                  (jj[None, :] // ah > jj[:, None] // ah)).astype(np.float32)
    prefix = np.triu(np.ones((NEW_LANES, NEW_LANES), np.float32))
    return later_key, later_page, prefix


def _fox_sample_body(npp, ah, dh, T, pt_ref, q_ref, kn_ref, vn_ref, alft_ref, uf_ref, wx_ref,
                     pre_ref, *rest):
    k_refs = rest[0:npp]
    v_refs = rest[npp:2 * npp]
    lf_refs = rest[2 * npp:3 * npp]
    o_ref, qbd_ref, m_ref, l_ref, acc_ref, tail_ref = rest[3 * npp:]
    g = pl.program_id(1)
    ng = pl.num_programs(1)
    aw = ah * dh
    page = k_refs[0].shape[1]
    R = T * ah

    lane = lax.broadcasted_iota(jnp.int32, (ah, aw), 1)
    head = lax.broadcasted_iota(jnp.int32, (ah, aw), 0)
    hmask = lane // dh == head

    @pl.when(g == 0)
    def _():
        q = q_ref[...] * (float(dh) ** -0.5)
        blocks = [jnp.where(hmask, jnp.broadcast_to(q[t:t + 1, :], (ah, aw)), 0.0)
                  for t in range(T)]
        qbd = jnp.concatenate(blocks, axis=0).astype(BF16)
        qbd_ref[...] = qbd
        a0, a1, a2 = _split3(alft_ref[...])
        pre = pre_ref[...]
        cn = _dot(a0, pre) + _dot(a1, pre) + _dot(a2, pre)
        s = _dot_nt(qbd, kn_ref[...].astype(BF16)) - jnp.concatenate([cn] * T, axis=0)
        col = lax.broadcasted_iota(jnp.int32, s.shape, 1)
        rt = lax.broadcasted_iota(jnp.int32, s.shape, 0) // ah
        s = jnp.where((col <= rt) & (col < T), s, -jnp.inf)
        m = jnp.max(s, axis=1, keepdims=True)
        pr = jnp.exp(s - m)
        m_ref[...] = m
        l_ref[...] = jnp.sum(pr, axis=1, keepdims=True)
        acc_ref[...] = _dot(pr.astype(BF16), vn_ref[...].astype(BF16))
        tail_ref[...] = jnp.zeros_like(tail_ref)

    lft = jnp.concatenate([r[...] for r in lf_refs], axis=0)
    p0, p1, p2 = _split3(lft)
    uf = uf_ref[...]
    wx = wx_ref[...]
    within = _dot(p0, uf) + _dot(p1, uf) + _dot(p2, uf)
    later = jnp.sum(_dot(wx, p0) + _dot(wx, p1) + _dot(wx, p2), axis=1, keepdims=True)
    tail = tail_ref[...]
    bias = within + later + jnp.concatenate([tail] * npp, axis=0)
    tot = jnp.sum(lft, axis=1, keepdims=True)
    step_tot = tot[0:ah]
    for j in range(1, npp):
        step_tot = step_tot + tot[j * ah:(j + 1) * ah]
    tail_ref[...] = tail + step_tot

    qbd = qbd_ref[...]
    s_parts = []
    for j in range(npp):
        kj = k_refs[j][...].astype(BF16)
        bj = bias[j * ah:(j + 1) * ah, :]
        s_parts.append(_dot(qbd, kj) + jnp.concatenate([bj] * T, axis=0))
    s = jnp.concatenate(s_parts, axis=1)
    m_old = m_ref[...]
    m_new = jnp.maximum(m_old, jnp.max(s, axis=1, keepdims=True))
    alpha = jnp.exp(m_old - m_new)
    pr = jnp.exp(s - m_new)
    l_ref[...] = alpha * l_ref[...] + jnp.sum(pr, axis=1, keepdims=True)
    m_ref[...] = m_new
    prb = pr.astype(BF16)
    acc = alpha * acc_ref[...]
    for j in range(npp):
        acc = acc + _dot_nt(prb[:, j * page:(j + 1) * page], v_refs[j][...].astype(BF16))
    acc_ref[...] = acc

    @pl.when(g == ng - 1)
    def _():
        out = acc / l_ref[...]
        rows = []
        for t in range(T):
            blk = jnp.where(hmask, out[t * ah:(t + 1) * ah, :], 0.0)
            rows.append(jnp.sum(blk, axis=0, keepdims=True))
        o_ref[...] = jnp.concatenate(rows, axis=0)


def _fox_sample(layer, page_table, q, kn, vn, alft, cache_kT, cache_vT, cache_lfT, uf, wx, pre, ah):
    Bd, T, aw = q.shape
    dh = aw // ah
    n_pages = page_table.shape[1]
    page = cache_kT.shape[3]
    npp = min(PAGES_PER_STEP, n_pages)
    ng = n_pages // npp
    R = T * ah

    def page_map(r):
        return lambda b, g, pt: (layer, pt[b, n_pages - (g + 1) * npp + r], 0, 0)

    seq = lambda shape: pl.BlockSpec((None,) + shape, lambda b, g, pt: (b, 0, 0))
    const = lambda shape: pl.BlockSpec(shape, lambda b, g, pt: (0,) * len(shape))
    in_specs = [seq((T, aw)), seq((NEW_LANES, aw)), seq((NEW_LANES, aw)), seq((ah, NEW_LANES)),
                const(uf.shape), const(wx.shape), const(pre.shape)]
    in_specs += [pl.BlockSpec((None, None, aw, page), page_map(r)) for r in range(npp)]
    in_specs += [pl.BlockSpec((None, None, aw, page), page_map(r)) for r in range(npp)]
    in_specs += [pl.BlockSpec((None, None, ah, page), page_map(r)) for r in range(npp)]
    grid_spec = pltpu.PrefetchScalarGridSpec(
        num_scalar_prefetch=1, grid=(Bd, ng), in_specs=in_specs,
        out_specs=pl.BlockSpec((None, T, aw), lambda b, g, pt: (b, 0, 0)),
        scratch_shapes=[pltpu.VMEM((R, aw), BF16), pltpu.VMEM((R, 1), F32),
                        pltpu.VMEM((R, 1), F32), pltpu.VMEM((R, aw), F32),
                        pltpu.VMEM((ah, 1), F32)])
    return pl.pallas_call(
        functools.partial(_fox_sample_body, npp, ah, dh, T),
        grid_spec=grid_spec, out_shape=jax.ShapeDtypeStruct((Bd, T, aw), F32),
        compiler_params=_cparams(2), name="fox_sample",
    )(page_table, q, kn, vn, alft, uf, wx, pre, *([cache_kT] * npp), *([cache_vT] * npp),
      *([cache_lfT] * npp))


def _hgrn_sample_body(nh, T, q_ref, k_ref, v_ref, lf_ref, s0_ref, o_ref, s1_ref):
    hd = s0_ref.shape[-1]
    outs = [[] for _ in range(T)]
    for h in range(nh):
        sl = slice(h * hd, (h + 1) * hd)
        q = [q_ref[t:t + 1, sl] for t in range(T)]
        k = [k_ref[t:t + 1, sl] for t in range(T)]
        v = [v_ref[t:t + 1, sl] for t in range(T)]
        cum = [lf_ref[0:1, sl]]
        for t in range(1, T):
            cum.append(cum[-1] + lf_ref[t:t + 1, sl])
        s0 = s0_ref[h]
        pad = [jnp.zeros((1, hd), F32)] * (8 - T)
        qh = jnp.concatenate([q[t] * jnp.exp(cum[t]) for t in range(T)] + pad, axis=0)
        o_inter = _dot(qh.astype(BF16), s0.astype(BF16))
        for t in range(T):
            o = o_inter[t:t + 1, :]
            for s in range(t + 1):
                a = jnp.sum(q[t] * k[s] * jnp.exp(cum[t] - cum[s]), axis=1, keepdims=True)
                o = o + a * v[s]
            outs[t].append(o)
        kh = jnp.concatenate([k[s] * jnp.exp(cum[T - 1] - cum[s]) for s in range(T)] + pad, axis=0)
        vv = jnp.concatenate(v + pad, axis=0)
        decay_col = jnp.broadcast_to(jnp.exp(cum[T - 1]), (hd, hd)).T
        s1_ref[h] = decay_col * s0 + _dot_tn(kh.astype(BF16), vv.astype(BF16))
    for t in range(T):
        o_ref[t:t + 1, :] = jnp.concatenate(outs[t], axis=1)


def _hgrn_sample(hq, hk, hv, hlf, s0, nh):
    Bd, T, hw = hq.shape
    hd = hw // nh
    seq = pl.BlockSpec((None, T, hw), lambda b: (b, 0, 0))
    st = pl.BlockSpec((None, nh, hd, hd), lambda b: (b, 0, 0, 0))
    return pl.pallas_call(
        functools.partial(_hgrn_sample_body, nh, T),
        grid=(Bd,), in_specs=[seq, seq, seq, seq, st], out_specs=(seq, st),
        out_shape=(jax.ShapeDtypeStruct((Bd, T, hw), F32),
                   jax.ShapeDtypeStruct((Bd, nh, hd, hd), F32)),
        compiler_params=_cparams(1), name="hgrn_sample",
    )(hq, hk, hv, hlf, s0)


def _final_norm_body(x_ref, g_ref, o_ref):
    o_ref[...] = _rms_rows(x_ref[...], g_ref[...])


def _final_norm_shifted_body(x_ref, g_ref, o_ref):
    o_ref[...] = _rms_rows(x_ref[0], g_ref[...])


def _final_norm_prompt(x, g, seq, tm):
    B, _, D = x.shape
    return pl.pallas_call(
        _final_norm_shifted_body, grid=(B, seq // tm),
        in_specs=[pl.BlockSpec((pl.Element(1), pl.Element(tm), pl.Element(D)),
                               lambda b, i: (b, pl.multiple_of(N_META + i * tm, 8), 0)),
                  _full((1, D))],
        out_specs=pl.BlockSpec((None, tm, D), lambda b, i: (b, i, 0)),
        out_shape=jax.ShapeDtypeStruct((B, seq, D), F32),
        compiler_params=_cparams(2), name="final_norm_prompt",
    )(x, g)


def _final_norm_sample(x, g):
    return pl.pallas_call(
        _final_norm_body, grid=(1,), in_specs=[_full(x.shape), _full(g.shape)],
        out_specs=_full(x.shape), out_shape=jax.ShapeDtypeStruct(x.shape, F32),
        compiler_params=_cparams(1), name="final_norm_sample",
    )(x, g)


def _expand_heads(vec, ah):
    rep = jnp.repeat(vec.astype(F32), BIAS_STRIDE)
    return jnp.pad(rep, (0, BIAS_LANES - ah * BIAS_STRIDE))[None, :]


def kernel(x_prompt, x_sample, cache_k, cache_v, cache_logf, state_hgrn, state_conv, page_table,
           meta_tokens, g_mix, w_in, b_f, hg_lb, g_att_out, g_hg_out, w_out, g_ffn, w_up,
           conv_w, conv_b, w_down, g_final):
    B, seq, D = x_prompt.shape
    Bd, T, _ = x_sample.shape
    depth = w_in.shape[0]
    ah = b_f.shape[1]
    aw = cache_k.shape[3] * cache_k.shape[4]
    hw = hg_lb.shape[1]
    nh = state_hgrn.shape[2]
    dff = w_down.shape[1]
    n_pool, page = cache_k.shape[1], cache_k.shape[2]
    L = N_META + seq
    tm = ROW_TILE
    Lp = -(-L // tm) * tm
    assert ah * BIAS_STRIDE <= BIAS_LANES and ah % 2 == 0 and dff % FF_TILE == 0
    assert page_table.shape[1] % min(PAGES_PER_STEP, page_table.shape[1]) == 0 and T <= 8

    sm = jax.nn.softmax(hg_lb.astype(F32), axis=0)
    lb_all = jnp.cumsum(sm, axis=0) - sm[0:1]

    tri = jnp.asarray(np.tril(np.ones((tm, tm), np.float32)), BF16)
    tri_c = jnp.asarray(np.tril(np.ones((HG_CHUNK, HG_CHUNK), np.float32)), BF16)
    lv_np, levels = _hgrn_tables()
    lv = jnp.asarray(lv_np, jnp.int32)
    npp = min(PAGES_PER_STEP, page_table.shape[1])
    uf, wx, pre = (jnp.asarray(a, BF16) for a in _sample_tables(page, ah, npp))

    dh = aw // ah
    cache_kT = cache_k.transpose(0, 1, 3, 4, 2).reshape(depth, n_pool, aw, page)
    cache_vT = cache_v.transpose(0, 1, 3, 4, 2).reshape(depth, n_pool, aw, page)
    cache_lfT = cache_logf.transpose(0, 1, 3, 2)

    xp = jnp.concatenate([jnp.broadcast_to(meta_tokens[None].astype(F32), (B, N_META, D)),
                          x_prompt, jnp.zeros((B, Lp - L, D), F32)], axis=1)
    xs = x_sample.reshape(Bd * T, D)

    outs = [[] for _ in range(10)]
    for l in range(depth):
        wi = w_in[l]
        o = 3 * aw + ah
        zf_cols = wi[:, 3 * aw:o]
        w_arr = jnp.concatenate(
            [wi[:, :3 * aw], wi[:, o:o + 4 * hw],
             jnp.pad(jnp.repeat(zf_cols, BIAS_STRIDE, axis=1),
                     ((0, 0), (0, BIAS_LANES - ah * BIAS_STRIDE))),
             jnp.pad(zf_cols, ((0, 0), (0, BIAS_LANES - ah)))], axis=1).astype(BF16)
        bfe = _expand_heads(b_f[l], ah)
        bfp = jnp.pad(b_f[l].astype(F32), (0, BIAS_LANES - ah))[None, :]
        lb = lb_all[l][None, :]
        g1 = g_mix[l][None, :].astype(F32)
        ga = g_att_out[l][None, :].astype(F32)
        gh = g_hg_out[l][None, :].astype(F32)
        gf = g_ffn[l][None, :].astype(F32)
        wo = w_out[l].astype(BF16)
        wu = w_up[l].astype(BF16)
        wd = w_down[l].astype(BF16)
        cw = conv_w[l].astype(F32)
        cb = conv_b[l][None, :].astype(F32)

        (qT, ktm, kb, vT, k_new, v_new, alf, hq, hk, hv, hlf, gate) = _in_proj_prompt(
            xp, g1, w_arr, bfe, bfp, lb, tri, L, aw, hw, ah, nh)
        att = _fox_prompt(qT, ktm, kb, vT)
        ho, s_p = _hgrn_prompt(hq, hk, hv, hlf, tri_c, lv, levels, L, nh)
        xp2, h2 = _merge(xp.reshape(B * Lp, D), att.reshape(B * Lp, aw), ho.reshape(B * Lp, hw),
                         gate.reshape(B * Lp, hw), ga, gh, gf, wo, nh, tm)
        xp, c_p = _convffn_prompt(xp2.reshape(B, Lp, D), h2.reshape(B, Lp, D), wu, cw, cb, wd, L)
        outs[0].append(k_new.reshape(B, L, ah, aw // ah))
        outs[1].append(v_new.reshape(B, L, ah, aw // ah))
        outs[2].append(alf)
        outs[3].append(s_p)
        outs[4].append(c_p)

        (sq, sk, sv, salf, shq, shk, shv, shlf, sgate) = _in_proj_sample(
            xs, g1, w_arr, bfe, bfp, lb, aw, hw, ah, nh)
        padn = ((0, 0), (0, NEW_LANES - T), (0, 0))
        kn = jnp.pad(sk.reshape(Bd, T, aw), padn)
        vn = jnp.pad(sv.reshape(Bd, T, aw), padn)
        alft = jnp.pad(salf.reshape(Bd, T, ah).transpose(0, 2, 1), ((0, 0), (0, 0), (0, NEW_LANES - T)))
        att_s = _fox_sample(l, page_table, sq.reshape(Bd, T, aw), kn, vn, alft,
                            cache_kT, cache_vT, cache_lfT, uf, wx, pre, ah)
        ho_s, s_s = _hgrn_sample(shq.reshape(Bd, T, hw), shk.reshape(Bd, T, hw),
                                 shv.reshape(Bd, T, hw), shlf.reshape(Bd, T, hw),
                                 state_hgrn[l].astype(F32), nh)
        xs2, h2s = _merge(xs, att_s.reshape(Bd * T, aw), ho_s.reshape(Bd * T, hw), sgate,
                          ga, gh, gf, wo, nh, Bd * T)
        prev = state_conv[l].astype(F32)
        zeros = jnp.zeros((Bd, T - 1, dff), F32)
        p1 = jnp.concatenate([prev[:, 1:2], zeros], axis=1).reshape(Bd * T, dff)
        p2 = jnp.concatenate([prev, zeros[:, :T - 2]], axis=1).reshape(Bd * T, dff)
        xs, a_s = _convffn_sample(xs2, h2s, wu, cw, cb, wd, p1, p2, T)
        outs[5].append(sk.reshape(Bd, T, ah, aw // ah))
        outs[6].append(sv.reshape(Bd, T, ah, aw // ah))
        outs[7].append(salf.reshape(Bd, T, ah))
        outs[8].append(s_s)
        outs[9].append(a_s.reshape(Bd, T, dff)[:, T - 2:])

    gfin = g_final[None, :].astype(F32)
    y_prompt = _final_norm_prompt(xp, gfin, seq, 512 if seq % 512 == 0 else seq)
    y_sample = _final_norm_sample(xs, gfin).reshape(Bd, T, D)
    return (y_prompt, y_sample) + tuple(jnp.stack(o) for o in outs)
```
